```python
import math
import jax, jax.numpy as jnp
from jax import lax
import numpy as np

D_MODEL = 1024
BATCH = 16
SEQ = 4096
DEPTH = 2

N_MIXERS = 2
HEAD_DIM = 64
ROPE_THETA = 500000.0
ROT_DIM = HEAD_DIM // 4

DIL_GROUPS = ((128, 1), (512, 4), (2048, 16))
DIL_HEADS = 4
N_DIL_HEADS = len(DIL_GROUPS) * DIL_HEADS
DIL_WIDTH = N_DIL_HEADS * HEAD_DIM
WIN_BLOCK = 64

MEM_LEN = 256
MEM_HEADS = 4
MEM_WIDTH = MEM_HEADS * HEAD_DIM

MLA_HEADS = 8
MLA_NOPE = 64
MLA_ROPE = 32
MLA_V = 64
Q_LORA = 256
KV_LORA = 128
MLA_Q_BLOCK = 128

N_GROUPS = 4
EXPERTS_PER_GROUP = 8
N_EXPERTS = N_GROUPS * EXPERTS_PER_GROUP
TOP_K = 2
D_EXPERT = 512
MOE_BLOCK = 128

LN_EPS = 1e-5
RMS_EPS = 1e-6
NEG_INF = -1e30
ALPHA = (2 * DEPTH) ** 0.25
BETA = (8 * DEPTH) ** -0.25
N_A_LAYERS = (DEPTH + 1) // 2
N_B_LAYERS = DEPTH // 2

A_IN_WIDTH = 3 * DIL_WIDTH + MEM_WIDTH
B_IN_WIDTH = Q_LORA + KV_LORA + MLA_ROPE + MEM_WIDTH
A_OUT_IN = DIL_HEADS * HEAD_DIM + MEM_WIDTH
B_OUT_IN = MLA_HEADS * MLA_V + MEM_WIDTH

kernel_name = "hybrid_dilated_mla_hmoe_encoder"


def layer_norm(x, g, b):
    xf = x.astype(jnp.float32)
    mu = xf.mean(-1, keepdims=True)
    var = jnp.square(xf - mu).mean(-1, keepdims=True)
    return ((xf - mu) * lax.rsqrt(var + LN_EPS) * g + b).astype(x.dtype)


def rms_norm(x, g):
    xf = x.astype(jnp.float32)
    return (xf * lax.rsqrt(jnp.square(xf).mean(-1, keepdims=True) + RMS_EPS) * g).astype(x.dtype)


def rope_tables(seq, dim):
    inv = ROPE_THETA ** (-jnp.arange(0, dim, 2, dtype=jnp.float32) / dim)
    ang = jnp.arange(seq, dtype=jnp.float32)[:, None] * inv[None, :]
    return jnp.cos(ang), jnp.sin(ang)


def apply_rope(t, cos, sin):
    half = t.shape[-1] // 2
    tf = t.astype(jnp.float32)
    t1, t2 = tf[..., :half], tf[..., half:]
    c, s = cos[None, :, None, :], sin[None, :, None, :]
    return jnp.concatenate([t1 * c - t2 * s, t2 * c + t1 * s], -1).astype(t.dtype)


def partial_rope(t, cos, sin):
    return jnp.concatenate([apply_rope(t[..., :ROT_DIM], cos, sin), t[..., ROT_DIM:]], -1)


def banded_attention(q, k, v, radius):
    n, L, h, dh = q.shape
    blk = WIN_BLOCK
    nb = -(-L // blk)
    Lp = nb * blk
    qb = jnp.pad(q, ((0, 0), (0, Lp - L), (0, 0), (0, 0))).reshape(n, nb, blk, h, dh)
    kv_pad = ((0, 0), (blk, Lp - L + blk), (0, 0), (0, 0))
    kp = jnp.pad(k, kv_pad).reshape(n, nb + 2, blk, h, dh)
    vp = jnp.pad(v, kv_pad).reshape(n, nb + 2, blk, h, dh)
    kw = jnp.concatenate([kp[:, :-2], kp[:, 1:-1], kp[:, 2:]], axis=2)
    vw = jnp.concatenate([vp[:, :-2], vp[:, 1:-1], vp[:, 2:]], axis=2)
    qpos = jnp.arange(nb)[:, None] * blk + jnp.arange(blk)[None, :]
    kpos = jnp.arange(nb)[:, None] * blk - blk + jnp.arange(3 * blk)[None, :]
    mask = ((jnp.abs(qpos[:, :, None] - kpos[:, None, :]) <= radius)
            & (kpos[:, None, :] >= 0) & (kpos[:, None, :] < L))
    s = jnp.einsum('nbqhd,nbkhd->nbhqk', qb, kw, preferred_element_type=jnp.float32) * (dh ** -0.5)
    s = jnp.where(mask[None, :, None], s, NEG_INF)
    lse = jax.nn.logsumexp(s, axis=-1)
    p = jnp.exp(s - lse[..., None])
    o = jnp.einsum('nbhqk,nbkhd->nbqhd', p.astype(v.dtype), vw)
    o = o.reshape(n, Lp, h, dh)[:, :L]
    lse = lse.transpose(0, 1, 3, 2).reshape(n, Lp, h)[:, :L]
    return o, lse


def dilated_mixer(q, k, v):
    B, S = q.shape[0], q.shape[1]
    outs, lses = [], []
    for g, (window, dil) in enumerate(DIL_GROUPS):
        sl = slice(g * DIL_HEADS, (g + 1) * DIL_HEADS)
        T = S // dil
        radius = window // (2 * dil)

        def to_sub(t):
            return t[:, :, sl].reshape(B, T, dil, DIL_HEADS, HEAD_DIM).transpose(0, 2, 1, 3, 4).reshape(B * dil, T, DIL_HEADS, HEAD_DIM)

        o, lse = banded_attention(to_sub(q), to_sub(k), to_sub(v), radius)
        outs.append(o.reshape(B, dil, T, DIL_HEADS, HEAD_DIM).transpose(0, 2, 1, 3, 4).reshape(B, S, DIL_HEADS, HEAD_DIM))
        lses.append(lse.reshape(B, dil, T, DIL_HEADS).transpose(0, 2, 1, 3).reshape(B, S, DIL_HEADS))
    w = jax.nn.softmax(jnp.stack(lses), axis=0)
    return jnp.einsum('gbsh,gbshd->bshd', w.astype(q.dtype), jnp.stack(outs))


def mla_mixer(c_q, c_kv, k_rope, q_norm_g, w_qb, kv_norm_g, w_kvb, cos, sin):
    B, S = c_q.shape[0], c_q.shape[1]
    q = (rms_norm(c_q, q_norm_g) @ w_qb).reshape(B, S, MLA_HEADS, MLA_NOPE + MLA_ROPE)
    q_nope = q[..., :MLA_NOPE]
    q_pe = apply_rope(q[..., MLA_NOPE:], cos, sin)
    kv = (rms_norm(c_kv, kv_norm_g) @ w_kvb).reshape(B, S, MLA_HEADS, MLA_NOPE + MLA_V)
    k_nope, v = kv[..., :MLA_NOPE], kv[..., MLA_NOPE:]
    k_pe = apply_rope(k_rope[:, :, None, :], cos, sin)[:, :, 0]
    scale = (MLA_NOPE + MLA_ROPE) ** -0.5
    nq = S // MLA_Q_BLOCK
    qn_b = q_nope.reshape(B, nq, MLA_Q_BLOCK, MLA_HEADS, MLA_NOPE).transpose(1, 0, 2, 3, 4)
    qp_b = q_pe.reshape(B, nq, MLA_Q_BLOCK, MLA_HEADS, MLA_ROPE).transpose(1, 0, 2, 3, 4)

    def one_block(args):
        qn, qp = args
        s = (jnp.einsum('bqhd,bkhd->bhqk', qn, k_nope, preferred_element_type=jnp.float32)
             + jnp.einsum('bqhr,bkr->bhqk', qp, k_pe, preferred_element_type=jnp.float32)) * scale
        p = jax.nn.softmax(s, axis=-1)
        return jnp.einsum('bhqk,bkhd->bqhd', p.astype(v.dtype), v)

    o = lax.map(one_block, (qn_b, qp_b))
    return o.transpose(1, 0, 2, 3, 4).reshape(B, S, MLA_HEADS, MLA_V)


def memory_attention(q, mem, w_kv):
    B, M = mem.shape[0], mem.shape[1]
    kv = (mem @ w_kv).reshape(B, M, 2, MEM_HEADS, HEAD_DIM)
    k, v = kv[:, :, 0], kv[:, :, 1]
    s = jnp.einsum('bshd,bmhd->bhsm', q, k, preferred_element_type=jnp.float32) * (HEAD_DIM ** -0.5)
    p = jax.nn.softmax(s, axis=-1)
    return jnp.einsum('bhsm,bmhd->bshd', p.astype(v.dtype), v)


def hierarchical_moe(x, w_grp, b_grp, w_exp_router, b_exp, w_gate, w_up, w_down):
    B, S, D = x.shape
    xt = x.reshape(-1, D)
    n = xt.shape[0]
    grp_logits = (xt @ w_grp).astype(jnp.float32) + b_grp
    g_idx = jnp.argmax(grp_logits, axis=-1)
    g_gate = jnp.take_along_axis(jax.nn.softmax(grp_logits, axis=-1), g_idx[:, None], axis=-1)
    exp_logits = ((xt @ w_exp_router).astype(jnp.float32) + b_exp).reshape(n, N_GROUPS, EXPERTS_PER_GROUP)
    in_grp = jnp.take_along_axis(exp_logits, g_idx[:, None, None], axis=1)[:, 0]
    top_v, top_i = lax.top_k(in_grp, TOP_K)
    gates = jax.nn.softmax(top_v, axis=-1) * g_gate
    eid = (g_idx[:, None] * EXPERTS_PER_GROUP + top_i).reshape(-1).astype(jnp.int32)
    a = n * TOP_K
    tok = jnp.arange(a, dtype=jnp.int32) // TOP_K
    gw = gates.reshape(-1)
    order = jnp.argsort(eid)
    eid_s, tok_s, gw_s = eid[order], tok[order], gw[order]
    counts = jnp.zeros((N_EXPERTS,), jnp.int32).at[eid].add(1)
    starts = jnp.cumsum(counts) - counts
    padded = (counts + MOE_BLOCK - 1) // MOE_BLOCK * MOE_BLOCK
    pends = jnp.cumsum(padded)
    pstarts = pends - padded
    dest = pstarts[eid_s] + (jnp.arange(a, dtype=jnp.int32) - starts[eid_s])
    n_slots = (-(-a // MOE_BLOCK) + N_EXPERTS) * MOE_BLOCK
    n_blocks = n_slots // MOE_BLOCK
    slot_tok = jnp.zeros((n_slots,), jnp.int32).at[dest].set(tok_s)
    slot_w = jnp.zeros((n_slots,), jnp.float32).at[dest].set(gw_s)
    blk_expert = jnp.minimum(jnp.searchsorted(pends, jnp.arange(n_blocks, dtype=jnp.int32) * MOE_BLOCK, side='right'), N_EXPERTS - 1)
    xs = xt[slot_tok].reshape(n_blocks, MOE_BLOCK, D)

    def expert_block(args):
        xb, e = args
        h = jax.nn.silu(xb @ w_gate[e]) * (xb @ w_up[e])
        return h @ w_down[e]

    ys = lax.map(expert_block, (xs, blk_expert)).reshape(n_slots, D)
    out = jax.ops.segment_sum(ys * slot_w[:, None].astype(ys.dtype), slot_tok, num_segments=n)
    return out.reshape(B, S, D)


def setup_inputs(seed: int = 0) -> dict:
    key = jax.random.key(seed)
    ks = iter(jax.random.split(key, 32))

    def nrm(shape, scale):
        return jax.random.normal(next(ks), shape, jnp.float32) * scale

    D = D_MODEL
    return {
        "x": nrm((BATCH, SEQ, D), 1.0),
        "mem": nrm((BATCH, MEM_LEN, D), 1.0),
        "a_w_in": nrm((N_A_LAYERS, D, A_IN_WIDTH), D ** -0.5),
        "a_w_out": nrm((N_A_LAYERS, A_OUT_IN, D), A_OUT_IN ** -0.5 * BETA),
        "b_w_in": nrm((N_B_LAYERS, D, B_IN_WIDTH), D ** -0.5),
        "b_q_norm": 1.0 + nrm((N_B_LAYERS, Q_LORA), 0.02),
        "b_w_qb": nrm((N_B_LAYERS, Q_LORA, MLA_HEADS * (MLA_NOPE + MLA_ROPE)), Q_LORA ** -0.5),
        "b_kv_norm": 1.0 + nrm((N_B_LAYERS, KV_LORA), 0.02),
        "b_w_kvb": nrm((N_B_LAYERS, KV_LORA, MLA_HEADS * (MLA_NOPE + MLA_V)), KV_LORA ** -0.5),
        "b_w_out": nrm((N_B_LAYERS, B_OUT_IN, D), B_OUT_IN ** -0.5 * BETA),
        "mem_w_kv": nrm((DEPTH, D, 2 * MEM_WIDTH), D ** -0.5),
        "ln_mix_g": 1.0 + nrm((DEPTH, D), 0.02),
        "ln_mix_b": nrm((DEPTH, D), 0.02),
        "ln_ffn_g": 1.0 + nrm((DEPTH, D), 0.02),
        "ln_ffn_b": nrm((DEPTH, D), 0.02),
        "router_w_grp": nrm((DEPTH, D, N_GROUPS), D ** -0.5),
        "router_b_grp": nrm((DEPTH, N_GROUPS), 0.01),
        "router_w_exp": nrm((DEPTH, D, N_EXPERTS), D ** -0.5),
        "router_b_exp": nrm((DEPTH, N_EXPERTS), 0.01),
        "exp_w_gate": nrm((DEPTH, N_EXPERTS, D, D_EXPERT), D ** -0.5),
        "exp_w_up": nrm((DEPTH, N_EXPERTS, D, D_EXPERT), D ** -0.5),
        "exp_w_down": nrm((DEPTH, N_EXPERTS, D_EXPERT, D), D_EXPERT ** -0.5 * BETA),
    }


def reference(x, mem, a_w_in, a_w_out, b_w_in, b_q_norm, b_w_qb, b_kv_norm, b_w_kvb, b_w_out,
              mem_w_kv, ln_mix_g, ln_mix_b, ln_ffn_g, ln_ffn_b, router_w_grp, router_b_grp,
              router_w_exp, router_b_exp, exp_w_gate, exp_w_up, exp_w_down):
    B, S, _ = x.shape
    cos_a, sin_a = rope_tables(S, ROT_DIM)
    cos_b, sin_b = rope_tables(S, MLA_ROPE)
    for i in range(DEPTH):
        j = i // N_MIXERS
        if i % N_MIXERS == 0:
            proj = x @ a_w_in[j]
            q, k, v, qm = jnp.split(proj, [DIL_WIDTH, 2 * DIL_WIDTH, 3 * DIL_WIDTH], axis=-1)
            q = partial_rope(q.reshape(B, S, N_DIL_HEADS, HEAD_DIM), cos_a, sin_a)
            k = partial_rope(k.reshape(B, S, N_DIL_HEADS, HEAD_DIM), cos_a, sin_a)
            v = v.reshape(B, S, N_DIL_HEADS, HEAD_DIM)
            mix = dilated_mixer(q, k, v).reshape(B, S, DIL_HEADS * HEAD_DIM)
            w_out = a_w_out[j]
        else:
            proj = x @ b_w_in[j]
            c_q, c_kv, k_rope, qm = jnp.split(proj, [Q_LORA, Q_LORA + KV_LORA, Q_LORA + KV_LORA + MLA_ROPE], axis=-1)
            mix = mla_mixer(c_q, c_kv, k_rope, b_q_norm[j], b_w_qb[j], b_kv_norm[j], b_w_kvb[j],
                            cos_b, sin_b).reshape(B, S, MLA_HEADS * MLA_V)
            w_out = b_w_out[j]
        mo = memory_attention(qm.reshape(B, S, MEM_HEADS, HEAD_DIM), mem, mem_w_kv[i]).reshape(B, S, MEM_WIDTH)
        y = jnp.concatenate([mix, mo], axis=-1) @ w_out
        x = layer_norm(ALPHA * x + y, ln_mix_g[i], ln_mix_b[i])
        f = hierarchical_moe(x, router_w_grp[i], router_b_grp[i], router_w_exp[i], router_b_exp[i],
                             exp_w_gate[i], exp_w_up[i], exp_w_down[i])
        x = layer_norm(ALPHA * x + f, ln_ffn_g[i], ln_ffn_b[i])
    return x
```

```python
import functools

import jax
import jax.numpy as jnp
from jax import lax
from jax.experimental import pallas as pl
from jax.experimental.pallas import tpu as pltpu

D_MODEL = 1024
HEAD_DIM = 64
ROPE_THETA = 500000.0
ROT_DIM = HEAD_DIM // 4

DIL_GROUPS = ((128, 1), (512, 4), (2048, 16))
DIL_HEADS = 4
GROUP_WIDTH = DIL_HEADS * HEAD_DIM
BAND_RADIUS = 64
BAND_Q = 128
BAND_KEYS = BAND_Q + 2 * BAND_RADIUS

MEM_HEADS = 4
MEM_WIDTH = MEM_HEADS * HEAD_DIM

MLA_HEADS = 8
MLA_NOPE = 64
MLA_ROPE = 32
MLA_V = 64
Q_LORA = 256
KV_LORA = 128
MLA_QK_PAD = 128

N_GROUPS = 4
EXPERTS_PER_GROUP = 8
N_EXPERTS = N_GROUPS * EXPERTS_PER_GROUP
TOP_K = 2
D_EXPERT = 512

LN_EPS = 1e-5
RMS_EPS = 1e-6
NEG_INF = -1e30
DEPTH = 2
ALPHA = (2 * DEPTH) ** 0.25

LANES = 128
ROW_TILE = 512
MOE_TILE = 512
MLA_TQ = 256
MLA_TK = 512
MEM_TQ = 512
VMEM_LIMIT = 48 * 1024 * 1024

BF16 = jnp.bfloat16
F32 = jnp.float32


def _params(n_axes):
    return pltpu.CompilerParams(dimension_semantics=("arbitrary",) * n_axes, vmem_limit_bytes=VMEM_LIMIT)


def _dot(a, b):
    return jnp.dot(a, b, preferred_element_type=F32)


def _dot_nt(a, b):
    return lax.dot_general(a, b, (((1,), (1,)), ((), ())), preferred_element_type=F32)


def _rope_lanes(t, c, s_lo, s_hi, half):
    return t * c + pltpu.roll(t, LANES - half, 1) * s_lo + pltpu.roll(t, half, 1) * s_hi


def _layer_norm(z, g, b):
    mu = jnp.mean(z, axis=-1, keepdims=True)
    zc = z - mu
    var = jnp.mean(zc * zc, axis=-1, keepdims=True)
    return zc * lax.rsqrt(var + LN_EPS) * g + b


def _a_proj_kernel(x_ref, w_ref, c_ref, slo_ref, shi_ref, *rest):
    outs, scr = rest[:-1], rest[-1]
    qm_ref = outs[-1]
    xb = x_ref[...].astype(BF16)
    c, slo, shi = c_ref[...], slo_ref[...], shi_ref[...]
    tm = xb.shape[0]
    for part in range(3):
        for g, (_, dil) in enumerate(DIL_GROUPS):
            col = (part * len(DIL_GROUPS) + g) * GROUP_WIDTH
            res = _dot(xb, w_ref[:, col:col + GROUP_WIDTH])
            o_ref = outs[part * len(DIL_GROUPS) + g]
            for j in range(GROUP_WIDTH // LANES):
                sl = slice(j * LANES, (j + 1) * LANES)
                half = res[:, sl]
                if part < 2:
                    half = _rope_lanes(half, c, slo, shi, ROT_DIM // 2)
                if part == 0:
                    half = half * (HEAD_DIM ** -0.5)
                if dil == 1:
                    o_ref[0, 0, :, sl] = half.astype(BF16)
                else:
                    scr[j] = half
                    for r in range(dil):
                        o_ref[0, r, :, sl] = scr[j, pl.ds(r, tm // dil, stride=dil), :].astype(BF16)
    col = 3 * len(DIL_GROUPS) * GROUP_WIDTH
    qm_ref[...] = (_dot(xb, w_ref[:, col:col + MEM_WIDTH]) * (HEAD_DIM ** -0.5)).astype(BF16)


def _a_projection(x2, w_in, tabs, B, S):
    n = x2.shape[0]
    tm = ROW_TILE
    per_seq = S // tm
    out_shapes, out_specs = [], []
    for _ in range(3):
        for _, dil in DIL_GROUPS:
            out_shapes.append(jax.ShapeDtypeStruct((B, dil, S // dil, GROUP_WIDTH), BF16))
            out_specs.append(pl.BlockSpec((1, dil, tm // dil, GROUP_WIDTH),
                                          lambda i: (i // per_seq, 0, i % per_seq, 0)))
    out_shapes.append(jax.ShapeDtypeStruct((n, MEM_WIDTH), BF16))
    out_specs.append(pl.BlockSpec((tm, MEM_WIDTH), lambda i: (i, 0)))
    tab_spec = pl.BlockSpec((tm, LANES), lambda i: (i % per_seq, 0))
    return pl.pallas_call(
        _a_proj_kernel,
        grid=(n // tm,),
        in_specs=[pl.BlockSpec((tm, D_MODEL), lambda i: (i, 0)),
                  pl.BlockSpec(w_in.shape, lambda i: (0, 0)),
                  tab_spec, tab_spec, tab_spec],
        out_specs=out_specs,
        out_shape=out_shapes,
        scratch_shapes=[pltpu.VMEM((GROUP_WIDTH // LANES, tm, LANES), F32)],
        compiler_params=_params(1),
        name="a_projection",
    )(x2, w_in, *tabs)


def _band_kernel(q_ref, k_ref, v_ref, o_ref, lse_ref, *, T):
    i = pl.program_id(1)
    start = pl.multiple_of(jnp.clip(i * BAND_Q - BAND_RADIUS, 0, T - BAND_KEYS), BAND_RADIUS)
    q = q_ref[0]
    k = k_ref[0, pl.ds(start, BAND_KEYS), :]
    v = v_ref[0, pl.ds(start, BAND_KEYS), :]
    qpos = i * BAND_Q + lax.broadcasted_iota(jnp.int32, (BAND_Q, BAND_KEYS), 0)
    kpos = start + lax.broadcasted_iota(jnp.int32, (BAND_Q, BAND_KEYS), 1)
    in_band = jnp.abs(qpos - kpos) <= BAND_RADIUS
    lane = lax.broadcasted_iota(jnp.int32, (BAND_Q, LANES), 1)
    for pair in range(GROUP_WIDTH // LANES):
        sl = slice(pair * LANES, (pair + 1) * LANES)
        qp, kp, vp = q[:, sl], k[:, sl], v[:, sl]
        o_pair = jnp.zeros((BAND_Q, LANES), F32)
        lse_pair = jnp.zeros((BAND_Q, LANES), F32)
        for hh in range(LANES // HEAD_DIM):
            own = (lane >= hh * HEAD_DIM) & (lane < (hh + 1) * HEAD_DIM)
            qh = jnp.where(own, qp, jnp.zeros_like(qp))
            s = jnp.where(in_band, _dot_nt(qh, kp), NEG_INF)
            m = jnp.max(s, axis=-1, keepdims=True)
            p = jnp.exp(s - m)
            l = jnp.sum(p, axis=-1, keepdims=True)
            o = _dot(p.astype(BF16), vp) / l
            o_pair = jnp.where(own, o, o_pair)
            lse_pair = jnp.where(own, m + jnp.log(l), lse_pair)
        o_ref[0, :, sl] = o_pair
        lse_ref[0, :, sl] = lse_pair


def _band_attention(q, k, v):
    NS, T, W = q.shape
    blk = pl.BlockSpec((1, BAND_Q, W), lambda s, i: (s, i, 0))
    seq = pl.BlockSpec((1, T, W), lambda s, i: (s, 0, 0))
    return pl.pallas_call(
        functools.partial(_band_kernel, T=T),
        grid=(NS, T // BAND_Q),
        in_specs=[blk, seq, seq],
        out_specs=[blk, blk],
        out_shape=[jax.ShapeDtypeStruct((NS, T, W), F32)] * 2,
        compiler_params=_params(2),
        name="band_attention",
    )(q, k, v)


def _a_merge_kernel(*refs):
    ng = len(DIL_GROUPS)
    o_refs, l_refs, out_ref = refs[:ng], refs[ng:2 * ng], refs[2 * ng]
    scr_o, scr_l = refs[2 * ng + 1], refs[2 * ng + 2]
    tm = out_ref.shape[0]
    os_, ls_ = [], []
    for g, (_, dil) in enumerate(DIL_GROUPS):
        if dil == 1:
            os_.append(o_refs[g][0, 0])
            ls_.append(l_refs[g][0, 0])
        else:
            nh = GROUP_WIDTH // LANES
            for r in range(dil):
                for j in range(nh):
                    sl = slice(j * LANES, (j + 1) * LANES)
                    scr_o[g * nh + j, pl.ds(r, tm // dil, stride=dil), :] = o_refs[g][0, r, :, sl]
                    scr_l[g * nh + j, pl.ds(r, tm // dil, stride=dil), :] = l_refs[g][0, r, :, sl]
            os_.append(jnp.concatenate([scr_o[g * nh + j] for j in range(nh)], axis=1))
            ls_.append(jnp.concatenate([scr_l[g * nh + j] for j in range(nh)], axis=1))
    m = functools.reduce(jnp.maximum, ls_)
    ws = [jnp.exp(l - m) for l in ls_]
    den = functools.reduce(jnp.add, ws)
    num = functools.reduce(jnp.add, [w * o for w, o in zip(ws, os_)])
    out_ref[...] = (num / den).astype(BF16)


def _a_merge(outs, lses, B, S):
    n = B * S
    tm = ROW_TILE
    per_seq = S // tm
    specs = [pl.BlockSpec((1, dil, tm // dil, GROUP_WIDTH), lambda i: (i // per_seq, 0, i % per_seq, 0))
             for _, dil in DIL_GROUPS]
    return pl.pallas_call(
        _a_merge_kernel,
        grid=(n // tm,),
        in_specs=specs + specs,
        out_specs=pl.BlockSpec((tm, GROUP_WIDTH), lambda i: (i, 0)),
        out_shape=jax.ShapeDtypeStruct((n, GROUP_WIDTH), BF16),
        scratch_shapes=[pltpu.VMEM((len(DIL_GROUPS) * GROUP_WIDTH // LANES, tm, LANES), F32)] * 2,
        compiler_params=_params(1),
        name="a_merge",
    )(*outs, *lses)


def _b_proj_kernel(x_ref, w_ref, qg_ref, kvg_ref, wq_ref, wk_ref, wv_ref, e_ref,
                   cq_ref, sqlo_ref, sqhi_ref, ck_ref, sklo_ref, skhi_ref,
                   q_out, k_out, v_out, qm_out):
    xb = x_ref[...].astype(BF16)
    proj = _dot(xb, w_ref[...])
    c_q = proj[:, :Q_LORA]
    qm = proj[:, Q_LORA:Q_LORA + MEM_WIDTH]
    c_kv = proj[:, Q_LORA + MEM_WIDTH:Q_LORA + MEM_WIDTH + KV_LORA]
    k_rope = proj[:, Q_LORA + MEM_WIDTH + KV_LORA:]
    qm_out[...] = (qm * (HEAD_DIM ** -0.5)).astype(BF16)

    qn = c_q * lax.rsqrt(jnp.mean(c_q * c_q, axis=-1, keepdims=True) + RMS_EPS) * qg_ref[...]
    kvn = c_kv * lax.rsqrt(jnp.mean(c_kv * c_kv, axis=-1, keepdims=True) + RMS_EPS) * kvg_ref[...]
    qn, kvn = qn.astype(BF16), kvn.astype(BF16)

    cq, sqlo, sqhi = cq_ref[...], sqlo_ref[...], sqhi_ref[...]
    scale = (MLA_NOPE + MLA_ROPE) ** -0.5
    for h in range(MLA_HEADS):
        sl = slice(h * MLA_QK_PAD, (h + 1) * MLA_QK_PAD)
        qh = _rope_lanes(_dot(qn, wq_ref[:, sl]), cq, sqlo, sqhi, MLA_ROPE // 2)
        q_out[:, sl] = (qh * scale).astype(BF16)

    k_pe = _rope_lanes(k_rope, ck_ref[...], sklo_ref[...], skhi_ref[...], MLA_ROPE // 2).astype(BF16)
    k_out[...] = (_dot(kvn, wk_ref[...]) + _dot(k_pe, e_ref[...])).astype(BF16)
    v_out[...] = _dot(kvn, wv_ref[...]).astype(BF16)


def _b_projection(x2, w_in, q_g, kv_g, wq, wk, wv, e_mat, tabs, S):
    n = x2.shape[0]
    tm = ROW_TILE
    per_seq = S // tm
    full = lambda a: pl.BlockSpec(a.shape, lambda i: (0,) * a.ndim)
    tab_spec = pl.BlockSpec((tm, LANES), lambda i: (i % per_seq, 0))
    row = lambda w: pl.BlockSpec((tm, w), lambda i: (i, 0))
    hw = MLA_HEADS * MLA_QK_PAD
    return pl.pallas_call(
        _b_proj_kernel,
        grid=(n // tm,),
        in_specs=[row(D_MODEL), full(w_in), full(q_g), full(kv_g), full(wq), full(wk), full(wv), full(e_mat)]
                 + [tab_spec] * 6,
        out_specs=[row(hw), row(hw), row(MLA_HEADS * MLA_V), row(MEM_WIDTH)],
        out_shape=[jax.ShapeDtypeStruct((n, hw), BF16), jax.ShapeDtypeStruct((n, hw), BF16),
                   jax.ShapeDtypeStruct((n, MLA_HEADS * MLA_V), BF16),
                   jax.ShapeDtypeStruct((n, MEM_WIDTH), BF16)],
        compiler_params=_params(1),
        name="b_projection",
    )(x2, w_in, q_g, kv_g, wq, wk, wv, e_mat, *tabs)


def _mla_kernel(q_ref, k_ref, v_ref, o_ref, *, S):
    tq = q_ref.shape[0]
    lane = lax.broadcasted_iota(jnp.int32, (tq, LANES), 1)
    heads = []
    for hh in range(LANES // MLA_V):
        sl = slice(hh * MLA_QK_PAD, (hh + 1) * MLA_QK_PAD)
        q = q_ref[:, sl]

        def body(kt, carry, sl=sl, q=q):
            m, l, acc = carry
            off = pl.multiple_of(kt * MLA_TK, MLA_TK)
            s = _dot_nt(q, k_ref[pl.ds(off, MLA_TK), sl])
            m_new = jnp.maximum(m, jnp.max(s, axis=-1, keepdims=True))
            a = jnp.exp(m - m_new)
            p = jnp.exp(s - m_new)
            l = a * l + jnp.sum(p, axis=-1, keepdims=True)
            acc = a * acc + _dot(p.astype(BF16), v_ref[pl.ds(off, MLA_TK), :])
            return m_new, l, acc

        init = (jnp.full((tq, 1), NEG_INF, F32), jnp.zeros((tq, 1), F32), jnp.zeros((tq, LANES), F32))
        _, l, acc = lax.fori_loop(0, S // MLA_TK, body, init)
        heads.append(acc / l)
    o_ref[...] = jnp.where(lane < MLA_V, heads[0], heads[1]).astype(BF16)


def _mla_attention(q, k, v, B, S):
    n = B * S
    per_seq = S // MLA_TQ
    pairs = MLA_HEADS // 2
    return pl.pallas_call(
        functools.partial(_mla_kernel, S=S),
        grid=(B, pairs, per_seq),
        in_specs=[pl.BlockSpec((MLA_TQ, 2 * MLA_QK_PAD), lambda b, p, i: (b * per_seq + i, p)),
                  pl.BlockSpec((S, 2 * MLA_QK_PAD), lambda b, p, i: (b, p)),
                  pl.BlockSpec((S, 2 * MLA_V), lambda b, p, i: (b, p))],
        out_specs=pl.BlockSpec((MLA_TQ, 2 * MLA_V), lambda b, p, i: (b * per_seq + i, p)),
        out_shape=jax.ShapeDtypeStruct((n, MLA_HEADS * MLA_V), BF16),
        compiler_params=_params(3),
        name="mla_attention",
    )(q, k, v)


def _cast_matmul_kernel(x_ref, w_ref, o_ref):
    o_ref[...] = _dot(x_ref[...].astype(BF16), w_ref[...]).astype(o_ref.dtype)


def _mem_kv(mem2, w):
    n, d = mem2.shape
    tm = min(ROW_TILE, n)
    return pl.pallas_call(
        _cast_matmul_kernel,
        grid=(n // tm,),
        in_specs=[pl.BlockSpec((tm, d), lambda i: (i, 0)), pl.BlockSpec(w.shape, lambda i: (0, 0))],
        out_specs=pl.BlockSpec((tm, w.shape[1]), lambda i: (i, 0)),
        out_shape=jax.ShapeDtypeStruct((n, w.shape[1]), BF16),
        compiler_params=_params(1),
        name="mem_kv",
    )(mem2, w)


def _mem_attn_kernel(q_ref, kv_ref, o_ref):
    tq = q_ref.shape[0]
    lane = lax.broadcasted_iota(jnp.int32, (tq, LANES), 1)
    for pair in range(MEM_WIDTH // LANES):
        sl = slice(pair * LANES, (pair + 1) * LANES)
        qp = q_ref[:, sl]
        kp = kv_ref[:, sl]
        vp = kv_ref[:, MEM_WIDTH + pair * LANES:MEM_WIDTH + (pair + 1) * LANES]
        o_pair = jnp.zeros((tq, LANES), F32)
        for hh in range(LANES // HEAD_DIM):
            own = (lane >= hh * HEAD_DIM) & (lane < (hh + 1) * HEAD_DIM)
            s = _dot_nt(jnp.where(own, qp, jnp.zeros_like(qp)), kp)
            m = jnp.max(s, axis=-1, keepdims=True)
            p = jnp.exp(s - m)
            l = jnp.sum(p, axis=-1, keepdims=True)
            o_pair = jnp.where(own, _dot(p.astype(BF16), vp) / l, o_pair)
        o_ref[:, sl] = o_pair.astype(BF16)


def _mem_attention(qm, kv, B, S, M):
    n = B * S
    per_seq = S // MEM_TQ
    return pl.pallas_call(
        _mem_attn_kernel,
        grid=(n // MEM_TQ,),
        in_specs=[pl.BlockSpec((MEM_TQ, MEM_WIDTH), lambda i: (i, 0)),
                  pl.BlockSpec((M, 2 * MEM_WIDTH), lambda i: (i // per_seq, 0))],
        out_specs=pl.BlockSpec((MEM_TQ, MEM_WIDTH), lambda i: (i, 0)),
        out_shape=jax.ShapeDtypeStruct((n, MEM_WIDTH), BF16),
        compiler_params=_params(1),
        name="mem_attention",
    )(qm, kv)


def _route(logits):
    lane = lax.broadcasted_iota(jnp.int32, logits.shape, 1).astype(F32)
    big = jnp.float32(1e9)
    is_grp = lane < N_GROUPS
    gl = jnp.where(is_grp, logits, NEG_INF)
    gmax = jnp.max(gl, axis=-1, keepdims=True)
    gidx = jnp.min(jnp.where(gl >= gmax, lane, big), axis=-1, keepdims=True)
    gsum = jnp.sum(jnp.where(is_grp, jnp.exp(gl - gmax), 0.0), axis=-1, keepdims=True)
    g_gate = 1.0 / gsum
    lo = N_GROUPS + EXPERTS_PER_GROUP * gidx
    el = jnp.where((lane >= lo) & (lane < lo + EXPERTS_PER_GROUP), logits, NEG_INF)
    v1 = jnp.max(el, axis=-1, keepdims=True)
    i1 = jnp.min(jnp.where(el >= v1, lane, big), axis=-1, keepdims=True)
    el2 = jnp.where(lane == i1, NEG_INF, el)
    v2 = jnp.max(el2, axis=-1, keepdims=True)
    i2 = jnp.min(jnp.where(el2 >= v2, lane, big), axis=-1, keepdims=True)
    e2 = jnp.exp(v2 - v1)
    den = 1.0 + e2
    gate1 = g_gate / den
    gate2 = g_gate * e2 / den
    return jnp.where(lane == 0, i1 - N_GROUPS,
                     jnp.where(lane == 1, i2 - N_GROUPS,
                               jnp.where(lane == 2, gate1, jnp.where(lane == 3, gate2, 0.0))))


def _post_kernel(x_ref, mix_ref, mo_ref, w1_ref, w2_ref, g_ref, b_ref, wr_ref, br_ref,
                 xn_ref, xb_ref, route_ref):
    y = _dot(mix_ref[...], w1_ref[...]) + _dot(mo_ref[...], w2_ref[...])
    xn = _layer_norm(ALPHA * x_ref[...] + y, g_ref[...], b_ref[...])
    xn_ref[...] = xn
    xb = xn.astype(BF16)
    xb_ref[...] = xb
    route_ref[...] = _route(_dot(xb, wr_ref[...]) + br_ref[...])


def _post_mixer(x2, mix, mo, w1, w2, g, b, wr, br):
    n = x2.shape[0]
    tm = ROW_TILE
    full = lambda a: pl.BlockSpec(a.shape, lambda i: (0,) * a.ndim)
    row = lambda w: pl.BlockSpec((tm, w), lambda i: (i, 0))
    return pl.pallas_call(
        _post_kernel,
        grid=(n // tm,),
        in_specs=[row(D_MODEL), row(mix.shape[1]), row(mo.shape[1]), full(w1), full(w2), full(g), full(b),
                  full(wr), full(br)],
        out_specs=[row(D_MODEL), row(D_MODEL), row(LANES)],
        out_shape=[jax.ShapeDtypeStruct((n, D_MODEL), F32), jax.ShapeDtypeStruct((n, D_MODEL), BF16),
                   jax.ShapeDtypeStruct((n, LANES), F32)],
        compiler_params=_params(1),
        name="post_mixer",
    )(x2, mix, mo, w1, w2, g, b, wr, br)


def _expert_kernel(blk_e_ref, n_used_ref, xs_ref, wg_ref, wu_ref, wd_ref, ys_ref):
    i = pl.program_id(0)

    @pl.when(i < n_used_ref[0])
    def _():
        xs = xs_ref[...]
        gate = _dot(xs, wg_ref[0])
        up = _dot(xs, wu_ref[0])
        h = (gate * jax.nn.sigmoid(gate) * up).astype(BF16)
        ys_ref[...] = _dot(h, wd_ref[0]).astype(BF16)

    @pl.when(i >= n_used_ref[0])
    def _():
        ys_ref[...] = jnp.zeros_like(ys_ref)


def _experts(xs, blk_expert, n_used, wg, wu, wd):
    n_slots = xs.shape[0]
    tm = MOE_TILE
    grid_spec = pltpu.PrefetchScalarGridSpec(
        num_scalar_prefetch=2,
        grid=(n_slots // tm,),
        in_specs=[pl.BlockSpec((tm, D_MODEL), lambda i, be, nu: (i, 0)),
                  pl.BlockSpec((1, D_MODEL, D_EXPERT), lambda i, be, nu: (be[i], 0, 0)),
                  pl.BlockSpec((1, D_MODEL, D_EXPERT), lambda i, be, nu: (be[i], 0, 0)),
                  pl.BlockSpec((1, D_EXPERT, D_MODEL), lambda i, be, nu: (be[i], 0, 0))],
        out_specs=pl.BlockSpec((tm, D_MODEL), lambda i, be, nu: (i, 0)),
    )
    return pl.pallas_call(
        _expert_kernel,
        grid_spec=grid_spec,
        out_shape=jax.ShapeDtypeStruct((n_slots, D_MODEL), BF16),
        compiler_params=_params(1),
        name="experts",
    )(blk_expert, n_used, xs, wg, wu, wd)


def _ffn_ln_kernel(xn_ref, y1_ref, y2_ref, route_ref, g_ref, b_ref, o_ref):
    route = route_ref[...]
    g1 = route[:, 2:3]
    g2 = route[:, 3:4]
    f = g1 * y1_ref[...].astype(F32) + g2 * y2_ref[...].astype(F32)
    o_ref[...] = _layer_norm(ALPHA * xn_ref[...] + f, g_ref[...], b_ref[...])


def _ffn_ln(xn, y1, y2, route, g, b):
    n = xn.shape[0]
    tm = ROW_TILE
    full = lambda a: pl.BlockSpec(a.shape, lambda i: (0,) * a.ndim)
    row = lambda w: pl.BlockSpec((tm, w), lambda i: (i, 0))
    return pl.pallas_call(
        _ffn_ln_kernel,
        grid=(n // tm,),
        in_specs=[row(D_MODEL), row(D_MODEL), row(D_MODEL), row(LANES), full(g), full(b)],
        out_specs=row(D_MODEL),
        out_shape=jax.ShapeDtypeStruct((n, D_MODEL), F32),
        compiler_params=_params(1),
        name="ffn_ln",
    )(xn, y1, y2, route, g, b)


def _moe(xn, xb, route, wg, wu, wd, ln_g, ln_b):
    n = xn.shape[0]
    a = n * TOP_K
    tm = MOE_TILE
    eid = route[:, :TOP_K].astype(jnp.int32).reshape(-1)
    order = jnp.argsort(eid)
    eid_s = eid[order]
    starts = jnp.searchsorted(eid_s, jnp.arange(N_EXPERTS, dtype=jnp.int32), side="left").astype(jnp.int32)
    counts = jnp.diff(jnp.concatenate([starts, jnp.array([a], jnp.int32)]))
    padded = (counts + tm - 1) // tm * tm
    pends = jnp.cumsum(padded)
    pstarts = pends - padded
    dest = pstarts[eid_s] + (jnp.arange(a, dtype=jnp.int32) - starts[eid_s])
    n_blocks = a // tm + N_EXPERTS
    n_slots = n_blocks * tm
    slot_tok = jnp.zeros((n_slots,), jnp.int32).at[dest].set(order // TOP_K)
    slot_of = jnp.zeros((a,), jnp.int32).at[order].set(dest).reshape(n, TOP_K)
    blk_expert = jnp.minimum(
        jnp.searchsorted(pends, jnp.arange(n_blocks, dtype=jnp.int32) * tm, side="right"),
        N_EXPERTS - 1).astype(jnp.int32)
    n_used = (pends[-1] // tm).astype(jnp.int32).reshape(1)
    xs = jnp.take(xb, slot_tok, axis=0)
    ys = _experts(xs, blk_expert, n_used, wg, wu, wd)
    y1 = jnp.take(ys, slot_of[:, 0], axis=0)
    y2 = jnp.take(ys, slot_of[:, 1], axis=0)
    return _ffn_ln(xn, y1, y2, route, ln_g, ln_b)


def _rope_tables(S, dim):
    inv = ROPE_THETA ** (-jnp.arange(0, dim, 2, dtype=F32) / dim)
    ang = jnp.arange(S, dtype=F32)[:, None] * inv[None, :]
    return jnp.cos(ang), jnp.sin(ang)


def _lane_tables(cos, sin, first, period):
    S, half = cos.shape
    z = jnp.zeros((S, half), F32)

    def build(rot_lo, rot_hi, fill):
        unit = jnp.concatenate([jnp.full((S, first), fill, F32), rot_lo, rot_hi,
                                jnp.full((S, period - first - 2 * half), fill, F32)], axis=1)
        return jnp.tile(unit, (1, LANES // period))

    return build(cos, cos, 1.0), build(-sin, z, 0.0), build(z, sin, 0.0)


def _router_weights(w_grp, b_grp, w_exp, b_exp):
    pad = LANES - N_GROUPS - N_EXPERTS
    wr = jnp.concatenate([w_grp, w_exp, jnp.zeros((D_MODEL, pad), F32)], axis=1).astype(BF16)
    br = jnp.concatenate([b_grp, b_exp, jnp.zeros((pad,), F32)]).reshape(1, LANES)
    return wr, br


def kernel(x, mem, a_w_in, a_w_out, b_w_in, b_q_norm, b_w_qb, b_kv_norm, b_w_kvb, b_w_out, mem_w_kv, ln_mix_g, ln_mix_b, ln_ffn_g, ln_ffn_b, router_w_grp, router_b_grp, router_w_exp, router_b_exp, exp_w_gate, exp_w_up, exp_w_down):
    B, S, D = x.shape
    M = mem.shape[1]
    n = B * S
    x2 = x.reshape(n, D)
    mem2 = mem.reshape(B * M, D)
    cos_a, sin_a = _rope_tables(S, ROT_DIM)
    cos_b, sin_b = _rope_tables(S, MLA_ROPE)
    tabs_a = _lane_tables(cos_a, sin_a, 0, HEAD_DIM)
    tabs_bq = _lane_tables(cos_b, sin_b, MLA_NOPE, MLA_QK_PAD)
    tabs_bk = _lane_tables(cos_b, sin_b, 0, LANES)
    row = lambda v: v.reshape(1, -1)

    for i in range(DEPTH):
        j = i // 2
        if i % 2 == 0:
            outs = _a_projection(x2, a_w_in[j].astype(BF16), tabs_a, B, S)
            ng = len(DIL_GROUPS)
            o_g, lse_g = [], []
            for g, (_, dil) in enumerate(DIL_GROUPS):
                T = S // dil
                sub = lambda t: t.reshape(B * dil, T, GROUP_WIDTH)
                o, lse = _band_attention(sub(outs[g]), sub(outs[ng + g]), sub(outs[2 * ng + g]))
                o_g.append(o.reshape(B, dil, T, GROUP_WIDTH))
                lse_g.append(lse.reshape(B, dil, T, GROUP_WIDTH))
            qm = outs[-1]
            mix = _a_merge(o_g, lse_g, B, S)
            w_out = a_w_out[j]
        else:
            w_in = b_w_in[j]
            c_q, c_kv, k_rope, w_qm = jnp.split(w_in, [Q_LORA, Q_LORA + KV_LORA, Q_LORA + KV_LORA + MLA_ROPE], axis=1)
            w_in_p = jnp.concatenate([c_q, w_qm, c_kv, k_rope, jnp.zeros((D, LANES - MLA_ROPE), F32)], axis=1)
            wq = jnp.pad(b_w_qb[j].reshape(Q_LORA, MLA_HEADS, MLA_NOPE + MLA_ROPE),
                         ((0, 0), (0, 0), (0, MLA_QK_PAD - MLA_NOPE - MLA_ROPE))).reshape(Q_LORA, -1)
            wkv = b_w_kvb[j].reshape(KV_LORA, MLA_HEADS, MLA_NOPE + MLA_V)
            wk = jnp.pad(wkv[:, :, :MLA_NOPE], ((0, 0), (0, 0), (0, MLA_QK_PAD - MLA_NOPE))).reshape(KV_LORA, -1)
            wv = wkv[:, :, MLA_NOPE:].reshape(KV_LORA, -1)
            rows = jnp.arange(LANES)[:, None]
            cols = jnp.arange(MLA_HEADS * MLA_QK_PAD)[None, :]
            e_mat = ((rows < MLA_ROPE) & (cols % MLA_QK_PAD == rows + MLA_NOPE)).astype(BF16)
            q, k, v, qm = _b_projection(x2, w_in_p.astype(BF16), row(b_q_norm[j]), row(b_kv_norm[j]),
                                        wq.astype(BF16), wk.astype(BF16), wv.astype(BF16), e_mat,
                                        tabs_bq + tabs_bk, S)
            mix = _mla_attention(q, k, v, B, S)
            w_out = b_w_out[j]
        kv_m = _mem_kv(mem2, mem_w_kv[i].astype(BF16))
        mo = _mem_attention(qm, kv_m, B, S, M)
        wm = mix.shape[1]
        wr, br = _router_weights(router_w_grp[i], router_b_grp[i], router_w_exp[i], router_b_exp[i])
        xn, xb, route = _post_mixer(x2, mix, mo, w_out[:wm].astype(BF16), w_out[wm:].astype(BF16),
                                    row(ln_mix_g[i]), row(ln_mix_b[i]), wr, br)
        x2 = _moe(xn, xb, route, exp_w_gate[i].astype(BF16), exp_w_up[i].astype(BF16),
                  exp_w_down[i].astype(BF16), row(ln_ffn_g[i]), row(ln_ffn_b[i]))
    return x2.reshape(B, S, D)
```

```python
import functools

import jax
import jax.numpy as jnp
from jax import lax
from jax.experimental import pallas as pl
from jax.experimental.pallas import tpu as pltpu

D_MODEL = 1024
HEAD_DIM = 64
ROPE_THETA = 500000.0
ROT_DIM = HEAD_DIM // 4

DIL_GROUPS = ((128, 1), (512, 4), (2048, 16))
DIL_HEADS = 4
GROUP_WIDTH = DIL_HEADS * HEAD_DIM
BAND_RADIUS = 64
BAND_Q = 128
BAND_KEYS = BAND_Q + 2 * BAND_RADIUS

MEM_HEADS = 4
MEM_WIDTH = MEM_HEADS * HEAD_DIM

MLA_HEADS = 8
MLA_NOPE = 64
MLA_ROPE = 32
MLA_V = 64
Q_LORA = 256
KV_LORA = 128
MLA_QK_PAD = 128

N_GROUPS = 4
EXPERTS_PER_GROUP = 8
N_EXPERTS = N_GROUPS * EXPERTS_PER_GROUP
TOP_K = 2
D_EXPERT = 512

LN_EPS = 1e-5
RMS_EPS = 1e-6
NEG_INF = -1e30
DEPTH = 2
ALPHA = (2 * DEPTH) ** 0.25
LOG2E = 1.4426950408889634

LANES = 128
ROW_TILE = 512
MOE_TILE = 512
MLA_TQ = 256
MLA_TK = 512
MEM_TQ = 512
VMEM_LIMIT = 48 * 1024 * 1024

BF16 = jnp.bfloat16
F32 = jnp.float32


def _params(n_axes):
    return pltpu.CompilerParams(dimension_semantics=("arbitrary",) * n_axes, vmem_limit_bytes=VMEM_LIMIT)


def _dot(a, b):
    return jnp.dot(a, b, preferred_element_type=F32)


def _dot_nt(a, b):
    return lax.dot_general(a, b, (((1,), (1,)), ((), ())), preferred_element_type=F32)


def _rope_lanes(t, c, s_lo, s_hi, half):
    return t * c + pltpu.roll(t, LANES - half, 1) * s_lo + pltpu.roll(t, half, 1) * s_hi


def _layer_norm(z, g, b):
    mu = jnp.mean(z, axis=-1, keepdims=True)
    zc = z - mu
    var = jnp.mean(zc * zc, axis=-1, keepdims=True)
    return zc * lax.rsqrt(var + LN_EPS) * g + b


def _a_proj_kernel(x_ref, w_ref, c_ref, slo_ref, shi_ref, *rest):
    outs, scr = rest[:-1], rest[-1]
    qm_ref = outs[-1]
    xb = x_ref[...].astype(BF16)
    c, slo, shi = c_ref[...], slo_ref[...], shi_ref[...]
    tm = xb.shape[0]
    for part in range(3):
        for g, (_, dil) in enumerate(DIL_GROUPS):
            col = (part * len(DIL_GROUPS) + g) * GROUP_WIDTH
            res = _dot(xb, w_ref[:, col:col + GROUP_WIDTH])
            o_ref = outs[part * len(DIL_GROUPS) + g]
            for j in range(GROUP_WIDTH // LANES):
                sl = slice(j * LANES, (j + 1) * LANES)
                half = res[:, sl]
                if part < 2:
                    half = _rope_lanes(half, c, slo, shi, ROT_DIM // 2)
                if part == 0:
                    half = half * (HEAD_DIM ** -0.5)
                if dil == 1:
                    o_ref[0, 0, :, sl] = half.astype(BF16)
                else:
                    scr[j] = half
                    for r in range(dil):
                        o_ref[0, r, :, sl] = scr[j, pl.ds(r, tm // dil, stride=dil), :].astype(BF16)
    col = 3 * len(DIL_GROUPS) * GROUP_WIDTH
    qm_ref[...] = (_dot(xb, w_ref[:, col:col + MEM_WIDTH]) * (HEAD_DIM ** -0.5)).astype(BF16)


def _a_projection(x2, w_in, tabs, B, S):
    n = x2.shape[0]
    tm = ROW_TILE
    per_seq = S // tm
    out_shapes, out_specs = [], []
    for _ in range(3):
        for _, dil in DIL_GROUPS:
            out_shapes.append(jax.ShapeDtypeStruct((B, dil, S // dil, GROUP_WIDTH), BF16))
            out_specs.append(pl.BlockSpec((1, dil, tm // dil, GROUP_WIDTH),
                                          lambda i: (i // per_seq, 0, i % per_seq, 0)))
    out_shapes.append(jax.ShapeDtypeStruct((n, MEM_WIDTH), BF16))
    out_specs.append(pl.BlockSpec((tm, MEM_WIDTH), lambda i: (i, 0)))
    tab_spec = pl.BlockSpec((tm, LANES), lambda i: (i % per_seq, 0))
    return pl.pallas_call(
        _a_proj_kernel,
        grid=(n // tm,),
        in_specs=[pl.BlockSpec((tm, D_MODEL), lambda i: (i, 0)),
                  pl.BlockSpec(w_in.shape, lambda i: (0, 0)),
                  tab_spec, tab_spec, tab_spec],
        out_specs=out_specs,
        out_shape=out_shapes,
        scratch_shapes=[pltpu.VMEM((GROUP_WIDTH // LANES, tm, LANES), F32)],
        compiler_params=_params(1),
        name="a_projection",
    )(x2, w_in, *tabs)


def _band_kernel(q_ref, k_ref, v_ref, o_ref, lse_ref, *, T):
    i = pl.program_id(1)
    start = pl.multiple_of(jnp.clip(i * BAND_Q - BAND_RADIUS, 0, T - BAND_KEYS), BAND_RADIUS)
    q = q_ref[0]
    k = k_ref[0, pl.ds(start, BAND_KEYS), :]
    v = v_ref[0, pl.ds(start, BAND_KEYS), :]
    qpos = i * BAND_Q + lax.broadcasted_iota(jnp.int32, (BAND_Q, BAND_KEYS), 0)
    kpos = start + lax.broadcasted_iota(jnp.int32, (BAND_Q, BAND_KEYS), 1)
    in_band = jnp.abs(qpos - kpos) <= BAND_RADIUS
    lane = lax.broadcasted_iota(jnp.int32, (BAND_Q, LANES), 1)
    for pair in range(GROUP_WIDTH // LANES):
        sl = slice(pair * LANES, (pair + 1) * LANES)
        qp, kp, vp = q[:, sl], k[:, sl], v[:, sl]
        o_pair = jnp.zeros((BAND_Q, LANES), F32)
        lse_pair = jnp.zeros((BAND_Q, LANES), F32)
        for hh in range(LANES // HEAD_DIM):
            own = (lane >= hh * HEAD_DIM) & (lane < (hh + 1) * HEAD_DIM)
            qh = jnp.where(own, qp, jnp.zeros_like(qp))
            s = jnp.where(in_band, _dot_nt(qh, kp), NEG_INF)
            m = jnp.max(s, axis=-1, keepdims=True)
            p = jnp.exp(s - m)
            l = jnp.sum(p, axis=-1, keepdims=True)
            o = _dot(p.astype(BF16), vp) / l
            o_pair = jnp.where(own, o, o_pair)
            lse_pair = jnp.where(own, m + jnp.log(l), lse_pair)
        o_ref[0, :, sl] = o_pair
        lse_ref[0, :, sl] = lse_pair


def _band_attention(q, k, v):
    NS, T, W = q.shape
    blk = pl.BlockSpec((1, BAND_Q, W), lambda s, i: (s, i, 0))
    seq = pl.BlockSpec((1, T, W), lambda s, i: (s, 0, 0))
    return pl.pallas_call(
        functools.partial(_band_kernel, T=T),
        grid=(NS, T // BAND_Q),
        in_specs=[blk, seq, seq],
        out_specs=[blk, blk],
        out_shape=[jax.ShapeDtypeStruct((NS, T, W), F32)] * 2,
        compiler_params=_params(2),
        name="band_attention",
    )(q, k, v)


def _a_merge_kernel(*refs):
    ng = len(DIL_GROUPS)
    o_refs, l_refs, out_ref = refs[:ng], refs[ng:2 * ng], refs[2 * ng]
    scr_o, scr_l = refs[2 * ng + 1], refs[2 * ng + 2]
    tm = out_ref.shape[0]
    os_, ls_ = [], []
    for g, (_, dil) in enumerate(DIL_GROUPS):
        if dil == 1:
            os_.append(o_refs[g][0, 0])
            ls_.append(l_refs[g][0, 0])
        else:
            nh = GROUP_WIDTH // LANES
            for r in range(dil):
                for j in range(nh):
                    sl = slice(j * LANES, (j + 1) * LANES)
                    scr_o[g * nh + j, pl.ds(r, tm // dil, stride=dil), :] = o_refs[g][0, r, :, sl]
                    scr_l[g * nh + j, pl.ds(r, tm // dil, stride=dil), :] = l_refs[g][0, r, :, sl]
            os_.append(jnp.concatenate([scr_o[g * nh + j] for j in range(nh)], axis=1))
            ls_.append(jnp.concatenate([scr_l[g * nh + j] for j in range(nh)], axis=1))
    m = functools.reduce(jnp.maximum, ls_)
    ws = [jnp.exp(l - m) for l in ls_]
    den = functools.reduce(jnp.add, ws)
    num = functools.reduce(jnp.add, [w * o for w, o in zip(ws, os_)])
    out_ref[...] = (num / den).astype(BF16)


def _a_merge(outs, lses, B, S):
    n = B * S
    tm = ROW_TILE
    per_seq = S // tm
    specs = [pl.BlockSpec((1, dil, tm // dil, GROUP_WIDTH), lambda i: (i // per_seq, 0, i % per_seq, 0))
             for _, dil in DIL_GROUPS]
    return pl.pallas_call(
        _a_merge_kernel,
        grid=(n // tm,),
        in_specs=specs + specs,
        out_specs=pl.BlockSpec((tm, GROUP_WIDTH), lambda i: (i, 0)),
        out_shape=jax.ShapeDtypeStruct((n, GROUP_WIDTH), BF16),
        scratch_shapes=[pltpu.VMEM((len(DIL_GROUPS) * GROUP_WIDTH // LANES, tm, LANES), F32)] * 2,
        compiler_params=_params(1),
        name="a_merge",
    )(*outs, *lses)


def _b_proj_kernel(x_ref, w_ref, qg_ref, kvg_ref, wq_ref, wk_ref, wv_ref, e_ref,
                   cq_ref, sqlo_ref, sqhi_ref, ck_ref, sklo_ref, skhi_ref,
                   q_out, k_out, v_out, qm_out):
    xb = x_ref[...].astype(BF16)
    proj = _dot(xb, w_ref[...])
    c_q = proj[:, :Q_LORA]
    qm = proj[:, Q_LORA:Q_LORA + MEM_WIDTH]
    c_kv = proj[:, Q_LORA + MEM_WIDTH:Q_LORA + MEM_WIDTH + KV_LORA]
    k_rope = proj[:, Q_LORA + MEM_WIDTH + KV_LORA:]
    qm_out[...] = (qm * (HEAD_DIM ** -0.5)).astype(BF16)

    qn = c_q * lax.rsqrt(jnp.mean(c_q * c_q, axis=-1, keepdims=True) + RMS_EPS) * qg_ref[...]
    kvn = c_kv * lax.rsqrt(jnp.mean(c_kv * c_kv, axis=-1, keepdims=True) + RMS_EPS) * kvg_ref[...]
    qn, kvn = qn.astype(BF16), kvn.astype(BF16)

    cq, sqlo, sqhi = cq_ref[...], sqlo_ref[...], sqhi_ref[...]
    scale = (MLA_NOPE + MLA_ROPE) ** -0.5 * LOG2E
    for h in range(MLA_HEADS):
        sl = slice(h * MLA_QK_PAD, (h + 1) * MLA_QK_PAD)
        qh = _rope_lanes(_dot(qn, wq_ref[:, sl]), cq, sqlo, sqhi, MLA_ROPE // 2)
        q_out[:, sl] = (qh * scale).astype(BF16)

    k_pe = _rope_lanes(k_rope, ck_ref[...], sklo_ref[...], skhi_ref[...], MLA_ROPE // 2).astype(BF16)
    k_out[...] = (_dot(kvn, wk_ref[...]) + _dot(k_pe, e_ref[...])).astype(BF16)
    v_out[...] = _dot_nt(wv_ref[...], kvn).astype(BF16)


def _b_projection(x2, w_in, q_g, kv_g, wq, wk, wv, e_mat, tabs, S):
    n = x2.shape[0]
    tm = ROW_TILE
    per_seq = S // tm
    full = lambda a: pl.BlockSpec(a.shape, lambda i: (0,) * a.ndim)
    tab_spec = pl.BlockSpec((tm, LANES), lambda i: (i % per_seq, 0))
    row = lambda w: pl.BlockSpec((tm, w), lambda i: (i, 0))
    hw = MLA_HEADS * MLA_QK_PAD
    return pl.pallas_call(
        _b_proj_kernel,
        grid=(n // tm,),
        in_specs=[row(D_MODEL), full(w_in), full(q_g), full(kv_g), full(wq), full(wk), full(wv), full(e_mat)]
                 + [tab_spec] * 6,
        out_specs=[row(hw), row(hw), pl.BlockSpec((MLA_HEADS * MLA_V, tm), lambda i: (0, i)), row(MEM_WIDTH)],
        out_shape=[jax.ShapeDtypeStruct((n, hw), BF16), jax.ShapeDtypeStruct((n, hw), BF16),
                   jax.ShapeDtypeStruct((MLA_HEADS * MLA_V, n), BF16),
                   jax.ShapeDtypeStruct((n, MEM_WIDTH), BF16)],
        compiler_params=_params(1),
        name="b_projection",
    )(x2, w_in, q_g, kv_g, wq, wk, wv, e_mat, *tabs)


def _mla_kernel(q_ref, k_ref, vt_ref, o_ref, s_scr, *, S):
    tq = q_ref.shape[0]
    tk = MLA_TK
    outs = []
    for hh in range(LANES // MLA_V):
        sl = slice(hh * MLA_QK_PAD, (hh + 1) * MLA_QK_PAD)
        q = q_ref[:, sl]
        mx8 = jnp.full((8, tq), NEG_INF, F32)
        for c in range(S // tk):
            s = _dot_nt(k_ref[c * tk:(c + 1) * tk, sl], q)
            s_scr[hh, c * tk:(c + 1) * tk, :] = s
            mx8 = jnp.maximum(mx8, jnp.max(s.reshape(tk // 8, 8, tq), axis=0))
        m = jnp.max(mx8, axis=0, keepdims=True)
        l8 = jnp.zeros((8, tq), F32)
        acc = jnp.zeros((MLA_V, tq), F32)
        for c in range(S // tk):
            p = jnp.exp2(s_scr[hh, c * tk:(c + 1) * tk, :] - m)
            l8 = l8 + jnp.sum(p.reshape(tk // 8, 8, tq), axis=0)
            acc = acc + _dot(vt_ref[hh * MLA_V:(hh + 1) * MLA_V, c * tk:(c + 1) * tk], p.astype(BF16))
        outs.append(acc / jnp.sum(l8, axis=0, keepdims=True))
    o_ref[...] = jnp.concatenate(outs, axis=0).T.astype(BF16)


def _mla_attention(q, k, vt, B, S):
    n = B * S
    per_seq = S // MLA_TQ
    pairs = MLA_HEADS // 2
    return pl.pallas_call(
        functools.partial(_mla_kernel, S=S),
        grid=(B, pairs, per_seq),
        in_specs=[pl.BlockSpec((MLA_TQ, 2 * MLA_QK_PAD), lambda b, p, i: (b * per_seq + i, p)),
                  pl.BlockSpec((S, 2 * MLA_QK_PAD), lambda b, p, i: (b, p)),
                  pl.BlockSpec((2 * MLA_V, S), lambda b, p, i: (p, b))],
        out_specs=pl.BlockSpec((MLA_TQ, 2 * MLA_V), lambda b, p, i: (b * per_seq + i, p)),
        out_shape=jax.ShapeDtypeStruct((n, MLA_HEADS * MLA_V), BF16),
        scratch_shapes=[pltpu.VMEM((2, S, MLA_TQ), F32)],
        compiler_params=_params(3),
        name="mla_attention",
    )(q, k, vt)


def _cast_matmul_kernel(x_ref, w_ref, o_ref):
    o_ref[...] = _dot(x_ref[...].astype(BF16), w_ref[...]).astype(o_ref.dtype)


def _mem_kv(mem2, w):
    n, d = mem2.shape
    tm = min(ROW_TILE, n)
    return pl.pallas_call(
        _cast_matmul_kernel,
        grid=(n // tm,),
        in_specs=[pl.BlockSpec((tm, d), lambda i: (i, 0)), pl.BlockSpec(w.shape, lambda i: (0, 0))],
        out_specs=pl.BlockSpec((tm, w.shape[1]), lambda i: (i, 0)),
        out_shape=jax.ShapeDtypeStruct((n, w.shape[1]), BF16),
        compiler_params=_params(1),
        name="mem_kv",
    )(mem2, w)


def _mem_attn_kernel(q_ref, kv_ref, o_ref):
    tq = q_ref.shape[0]
    lane = lax.broadcasted_iota(jnp.int32, (tq, LANES), 1)
    for pair in range(MEM_WIDTH // LANES):
        sl = slice(pair * LANES, (pair + 1) * LANES)
        qp = q_ref[:, sl]
        kp = kv_ref[:, sl]
        vp = kv_ref[:, MEM_WIDTH + pair * LANES:MEM_WIDTH + (pair + 1) * LANES]
        o_pair = jnp.zeros((tq, LANES), F32)
        for hh in range(LANES // HEAD_DIM):
            own = (lane >= hh * HEAD_DIM) & (lane < (hh + 1) * HEAD_DIM)
            s = _dot_nt(jnp.where(own, qp, jnp.zeros_like(qp)), kp)
            m = jnp.max(s, axis=-1, keepdims=True)
            p = jnp.exp(s - m)
            l = jnp.sum(p, axis=-1, keepdims=True)
            o_pair = jnp.where(own, _dot(p.astype(BF16), vp) / l, o_pair)
        o_ref[:, sl] = o_pair.astype(BF16)


def _mem_attention(qm, kv, B, S, M):
    n = B * S
    per_seq = S // MEM_TQ
    return pl.pallas_call(
        _mem_attn_kernel,
        grid=(n // MEM_TQ,),
        in_specs=[pl.BlockSpec((MEM_TQ, MEM_WIDTH), lambda i: (i, 0)),
                  pl.BlockSpec((M, 2 * MEM_WIDTH), lambda i: (i // per_seq, 0))],
        out_specs=pl.BlockSpec((MEM_TQ, MEM_WIDTH), lambda i: (i, 0)),
        out_shape=jax.ShapeDtypeStruct((n, MEM_WIDTH), BF16),
        compiler_params=_params(1),
        name="mem_attention",
    )(qm, kv)


def _route(logits):
    lane = lax.broadcasted_iota(jnp.int32, logits.shape, 1).astype(F32)
    big = jnp.float32(1e9)
    is_grp = lane < N_GROUPS
    gl = jnp.where(is_grp, logits, NEG_INF)
    gmax = jnp.max(gl, axis=-1, keepdims=True)
    gidx = jnp.min(jnp.where(gl >= gmax, lane, big), axis=-1, keepdims=True)
    gsum = jnp.sum(jnp.where(is_grp, jnp.exp(gl - gmax), 0.0), axis=-1, keepdims=True)
    g_gate = 1.0 / gsum
    lo = N_GROUPS + EXPERTS_PER_GROUP * gidx
    el = jnp.where((lane >= lo) & (lane < lo + EXPERTS_PER_GROUP), logits, NEG_INF)
    v1 = jnp.max(el, axis=-1, keepdims=True)
    i1 = jnp.min(jnp.where(el >= v1, lane, big), axis=-1, keepdims=True)
    el2 = jnp.where(lane == i1, NEG_INF, el)
    v2 = jnp.max(el2, axis=-1, keepdims=True)
    i2 = jnp.min(jnp.where(el2 >= v2, lane, big), axis=-1, keepdims=True)
    e2 = jnp.exp(v2 - v1)
    den = 1.0 + e2
    gate1 = g_gate / den
    gate2 = g_gate * e2 / den
    return jnp.where(lane == 0, i1 - N_GROUPS,
                     jnp.where(lane == 1, i2 - N_GROUPS,
                               jnp.where(lane == 2, gate1, jnp.where(lane == 3, gate2, 0.0))))


def _post_kernel(x_ref, mix_ref, mo_ref, w1_ref, w2_ref, g_ref, b_ref, wr_ref, br_ref,
                 xn_ref, xb_ref, route_ref):
    y = _dot(mix_ref[...], w1_ref[...]) + _dot(mo_ref[...], w2_ref[...])
    xn = _layer_norm(ALPHA * x_ref[...] + y, g_ref[...], b_ref[...])
    xn_ref[...] = xn
    xb = xn.astype(BF16)
    xb_ref[...] = xb
    route_ref[...] = _route(_dot(xb, wr_ref[...]) + br_ref[...])


def _post_mixer(x2, mix, mo, w1, w2, g, b, wr, br):
    n = x2.shape[0]
    tm = ROW_TILE
    full = lambda a: pl.BlockSpec(a.shape, lambda i: (0,) * a.ndim)
    row = lambda w: pl.BlockSpec((tm, w), lambda i: (i, 0))
    return pl.pallas_call(
        _post_kernel,
        grid=(n // tm,),
        in_specs=[row(D_MODEL), row(mix.shape[1]), row(mo.shape[1]), full(w1), full(w2), full(g), full(b),
                  full(wr), full(br)],
        out_specs=[row(D_MODEL), row(D_MODEL), row(LANES)],
        out_shape=[jax.ShapeDtypeStruct((n, D_MODEL), F32), jax.ShapeDtypeStruct((n, D_MODEL), BF16),
                   jax.ShapeDtypeStruct((n, LANES), F32)],
        compiler_params=_params(1),
        name="post_mixer",
    )(x2, mix, mo, w1, w2, g, b, wr, br)


def _expert_kernel(blk_e_ref, n_used_ref, xs_ref, wg_ref, wu_ref, wd_ref, ys_ref):
    i = pl.program_id(0)

    @pl.when(i < n_used_ref[0])
    def _():
        xs = xs_ref[...]
        gate = _dot(xs, wg_ref[0])
        up = _dot(xs, wu_ref[0])
        h = (gate * jax.nn.sigmoid(gate) * up).astype(BF16)
        ys_ref[...] = _dot(h, wd_ref[0]).astype(BF16)

    @pl.when(i >= n_used_ref[0])
    def _():
        ys_ref[...] = jnp.zeros_like(ys_ref)


def _experts(xs, blk_expert, n_used, wg, wu, wd):
    n_slots = xs.shape[0]
    tm = MOE_TILE
    grid_spec = pltpu.PrefetchScalarGridSpec(
        num_scalar_prefetch=2,
        grid=(n_slots // tm,),
        in_specs=[pl.BlockSpec((tm, D_MODEL), lambda i, be, nu: (i, 0)),
                  pl.BlockSpec((1, D_MODEL, D_EXPERT), lambda i, be, nu: (be[i], 0, 0)),
                  pl.BlockSpec((1, D_MODEL, D_EXPERT), lambda i, be, nu: (be[i], 0, 0)),
                  pl.BlockSpec((1, D_EXPERT, D_MODEL), lambda i, be, nu: (be[i], 0, 0))],
        out_specs=pl.BlockSpec((tm, D_MODEL), lambda i, be, nu: (i, 0)),
    )
    return pl.pallas_call(
        _expert_kernel,
        grid_spec=grid_spec,
        out_shape=jax.ShapeDtypeStruct((n_slots, D_MODEL), BF16),
        compiler_params=_params(1),
        name="experts",
    )(blk_expert, n_used, xs, wg, wu, wd)


def _ffn_ln_kernel(xn_ref, y1_ref, y2_ref, route_ref, g_ref, b_ref, o_ref):
    route = route_ref[...]
    g1 = route[:, 2:3]
    g2 = route[:, 3:4]
    f = g1 * y1_ref[...].astype(F32) + g2 * y2_ref[...].astype(F32)
    o_ref[...] = _layer_norm(ALPHA * xn_ref[...] + f, g_ref[...], b_ref[...])


def _ffn_ln(xn, y1, y2, route, g, b):
    n = xn.shape[0]
    tm = ROW_TILE
    full = lambda a: pl.BlockSpec(a.shape, lambda i: (0,) * a.ndim)
    row = lambda w: pl.BlockSpec((tm, w), lambda i: (i, 0))
    return pl.pallas_call(
        _ffn_ln_kernel,
        grid=(n // tm,),
        in_specs=[row(D_MODEL), row(D_MODEL), row(D_MODEL), row(LANES), full(g), full(b)],
        out_specs=row(D_MODEL),
        out_shape=jax.ShapeDtypeStruct((n, D_MODEL), F32),
        compiler_params=_params(1),
        name="ffn_ln",
    )(xn, y1, y2, route, g, b)


def _moe(xn, xb, route, wg, wu, wd, ln_g, ln_b):
    n = xn.shape[0]
    a = n * TOP_K
    tm = MOE_TILE
    eid = route[:, :TOP_K].astype(jnp.int32).reshape(-1)
    order = jnp.argsort(eid)
    eid_s = eid[order]
    starts = jnp.searchsorted(eid_s, jnp.arange(N_EXPERTS, dtype=jnp.int32), side="left").astype(jnp.int32)
    counts = jnp.diff(jnp.concatenate([starts, jnp.array([a], jnp.int32)]))
    padded = (counts + tm - 1) // tm * tm
    pends = jnp.cumsum(padded)
    pstarts = pends - padded
    dest = pstarts[eid_s] + (jnp.arange(a, dtype=jnp.int32) - starts[eid_s])
    n_blocks = a // tm + N_EXPERTS
    n_slots = n_blocks * tm
    slot_tok = jnp.zeros((n_slots,), jnp.int32).at[dest].set(order // TOP_K)
    slot_of = jnp.zeros((a,), jnp.int32).at[order].set(dest).reshape(n, TOP_K)
    blk_expert = jnp.minimum(
        jnp.searchsorted(pends, jnp.arange(n_blocks, dtype=jnp.int32) * tm, side="right"),
        N_EXPERTS - 1).astype(jnp.int32)
    n_used = (pends[-1] // tm).astype(jnp.int32).reshape(1)
    xs = jnp.take(xb, slot_tok, axis=0)
    ys = _experts(xs, blk_expert, n_used, wg, wu, wd)
    y1 = jnp.take(ys, slot_of[:, 0], axis=0)
    y2 = jnp.take(ys, slot_of[:, 1], axis=0)
    return _ffn_ln(xn, y1, y2, route, ln_g, ln_b)


def _rope_tables(S, dim):
    inv = ROPE_THETA ** (-jnp.arange(0, dim, 2, dtype=F32) / dim)
    ang = jnp.arange(S, dtype=F32)[:, None] * inv[None, :]
    return jnp.cos(ang), jnp.sin(ang)


def _lane_tables(cos, sin, first, period):
    S, half = cos.shape
    z = jnp.zeros((S, half), F32)

    def build(rot_lo, rot_hi, fill):
        unit = jnp.concatenate([jnp.full((S, first), fill, F32), rot_lo, rot_hi,
                                jnp.full((S, period - first - 2 * half), fill, F32)], axis=1)
        return jnp.tile(unit, (1, LANES // period))

    return build(cos, cos, 1.0), build(-sin, z, 0.0), build(z, sin, 0.0)


def _router_weights(w_grp, b_grp, w_exp, b_exp):
    pad = LANES - N_GROUPS - N_EXPERTS
    wr = jnp.concatenate([w_grp, w_exp, jnp.zeros((D_MODEL, pad), F32)], axis=1).astype(BF16)
    br = jnp.concatenate([b_grp, b_exp, jnp.zeros((pad,), F32)]).reshape(1, LANES)
    return wr, br


def kernel(x, mem, a_w_in, a_w_out, b_w_in, b_q_norm, b_w_qb, b_kv_norm, b_w_kvb, b_w_out, mem_w_kv, ln_mix_g, ln_mix_b, ln_ffn_g, ln_ffn_b, router_w_grp, router_b_grp, router_w_exp, router_b_exp, exp_w_gate, exp_w_up, exp_w_down):
    B, S, D = x.shape
    M = mem.shape[1]
    n = B * S
    x2 = x.reshape(n, D)
    mem2 = mem.reshape(B * M, D)
    cos_a, sin_a = _rope_tables(S, ROT_DIM)
    cos_b, sin_b = _rope_tables(S, MLA_ROPE)
    tabs_a = _lane_tables(cos_a, sin_a, 0, HEAD_DIM)
    tabs_bq = _lane_tables(cos_b, sin_b, MLA_NOPE, MLA_QK_PAD)
    tabs_bk = _lane_tables(cos_b, sin_b, 0, LANES)
    row = lambda v: v.reshape(1, -1)

    for i in range(DEPTH):
        j = i // 2
        if i % 2 == 0:
            outs = _a_projection(x2, a_w_in[j].astype(BF16), tabs_a, B, S)
            ng = len(DIL_GROUPS)
            o_g, lse_g = [], []
            for g, (_, dil) in enumerate(DIL_GROUPS):
                T = S // dil
                sub = lambda t: t.reshape(B * dil, T, GROUP_WIDTH)
                o, lse = _band_attention(sub(outs[g]), sub(outs[ng + g]), sub(outs[2 * ng + g]))
                o_g.append(o.reshape(B, dil, T, GROUP_WIDTH))
                lse_g.append(lse.reshape(B, dil, T, GROUP_WIDTH))
            qm = outs[-1]
            mix = _a_merge(o_g, lse_g, B, S)
            w_out = a_w_out[j]
        else:
            w_in = b_w_in[j]
            c_q, c_kv, k_rope, w_qm = jnp.split(w_in, [Q_LORA, Q_LORA + KV_LORA, Q_LORA + KV_LORA + MLA_ROPE], axis=1)
            w_in_p = jnp.concatenate([c_q, w_qm, c_kv, k_rope, jnp.zeros((D, LANES - MLA_ROPE), F32)], axis=1)
            wq = jnp.pad(b_w_qb[j].reshape(Q_LORA, MLA_HEADS, MLA_NOPE + MLA_ROPE),
                         ((0, 0), (0, 0), (0, MLA_QK_PAD - MLA_NOPE - MLA_ROPE))).reshape(Q_LORA, -1)
            wkv = b_w_kvb[j].reshape(KV_LORA, MLA_HEADS, MLA_NOPE + MLA_V)
            wk = jnp.pad(wkv[:, :, :MLA_NOPE], ((0, 0), (0, 0), (0, MLA_QK_PAD - MLA_NOPE))).reshape(KV_LORA, -1)
            wv = wkv[:, :, MLA_NOPE:].reshape(KV_LORA, -1).T
            rows = jnp.arange(LANES)[:, None]
            cols = jnp.arange(MLA_HEADS * MLA_QK_PAD)[None, :]
            e_mat = ((rows < MLA_ROPE) & (cols % MLA_QK_PAD == rows + MLA_NOPE)).astype(BF16)
            q, k, v, qm = _b_projection(x2, w_in_p.astype(BF16), row(b_q_norm[j]), row(b_kv_norm[j]),
                                        wq.astype(BF16), wk.astype(BF16), wv.astype(BF16), e_mat,
                                        tabs_bq + tabs_bk, S)
            mix = _mla_attention(q, k, v, B, S)
            w_out = b_w_out[j]
        kv_m = _mem_kv(mem2, mem_w_kv[i].astype(BF16))
        mo = _mem_attention(qm, kv_m, B, S, M)
        wm = mix.shape[1]
        wr, br = _router_weights(router_w_grp[i], router_b_grp[i], router_w_exp[i], router_b_exp[i])
        xn, xb, route = _post_mixer(x2, mix, mo, w_out[:wm].astype(BF16), w_out[wm:].astype(BF16),
                                    row(ln_mix_g[i]), row(ln_mix_b[i]), wr, br)
        x2 = _moe(xn, xb, route, exp_w_gate[i].astype(BF16), exp_w_up[i].astype(BF16),
                  exp_w_down[i].astype(BF16), row(ln_ffn_g[i]), row(ln_ffn_b[i]))
    return x2.reshape(B, S, D)
```

```python
import functools

import jax
import jax.numpy as jnp
from jax import lax
from jax.experimental import pallas as pl
from jax.experimental.pallas import tpu as pltpu

D_MODEL = 1024
HEAD_DIM = 64
ROPE_THETA = 500000.0
ROT_DIM = HEAD_DIM // 4

DIL_GROUPS = ((128, 1), (512, 4), (2048, 16))
DIL_HEADS = 4
GROUP_WIDTH = DIL_HEADS * HEAD_DIM
BAND_RADIUS = 64
BAND_Q = 128
BAND_KEYS = BAND_Q + 2 * BAND_RADIUS

MEM_HEADS = 4
MEM_WIDTH = MEM_HEADS * HEAD_DIM

MLA_HEADS = 8
MLA_NOPE = 64
MLA_ROPE = 32
MLA_V = 64
Q_LORA = 256
KV_LORA = 128
MLA_QK_PAD = 128

N_GROUPS = 4
EXPERTS_PER_GROUP = 8
N_EXPERTS = N_GROUPS * EXPERTS_PER_GROUP
TOP_K = 2
D_EXPERT = 512

LN_EPS = 1e-5
RMS_EPS = 1e-6
NEG_INF = -1e30
DEPTH = 2
ALPHA = (2 * DEPTH) ** 0.25
LOG2E = 1.4426950408889634

LANES = 128
ROW_TILE = 512
MOE_TILE = 512
MLA_TQ = 256
MLA_TK = 512
MEM_TQ = 512
VMEM_LIMIT = 48 * 1024 * 1024

BF16 = jnp.bfloat16
F32 = jnp.float32


def _params(n_axes):
    return pltpu.CompilerParams(dimension_semantics=("arbitrary",) * n_axes, vmem_limit_bytes=VMEM_LIMIT)


def _dot(a, b):
    return jnp.dot(a, b, preferred_element_type=F32)


def _dot_nt(a, b):
    return lax.dot_general(a, b, (((1,), (1,)), ((), ())), preferred_element_type=F32)


def _rope_lanes(t, c, s_lo, s_hi, half):
    return t * c + pltpu.roll(t, LANES - half, 1) * s_lo + pltpu.roll(t, half, 1) * s_hi


def _layer_norm(z, g, b):
    mu = jnp.mean(z, axis=-1, keepdims=True)
    zc = z - mu
    var = jnp.mean(zc * zc, axis=-1, keepdims=True)
    return zc * lax.rsqrt(var + LN_EPS) * g + b


def _a_proj_kernel(x_ref, w_ref, c_ref, slo_ref, shi_ref, *rest):
    outs, scr = rest[:-1], rest[-1]
    qm_ref = outs[-1]
    xb = x_ref[...].astype(BF16)
    c, slo, shi = c_ref[...], slo_ref[...], shi_ref[...]
    tm = xb.shape[0]
    for part in range(3):
        for g, (_, dil) in enumerate(DIL_GROUPS):
            col = (part * len(DIL_GROUPS) + g) * GROUP_WIDTH
            res = _dot(xb, w_ref[:, col:col + GROUP_WIDTH])
            o_ref = outs[part * len(DIL_GROUPS) + g]
            for j in range(GROUP_WIDTH // LANES):
                sl = slice(j * LANES, (j + 1) * LANES)
                half = res[:, sl]
                if part < 2:
                    half = _rope_lanes(half, c, slo, shi, ROT_DIM // 2)
                if part == 0:
                    half = half * (HEAD_DIM ** -0.5)
                if dil == 1:
                    o_ref[0, 0, :, sl] = half.astype(BF16)
                else:
                    scr[j] = half
                    for r in range(dil):
                        o_ref[0, r, :, sl] = scr[j, pl.ds(r, tm // dil, stride=dil), :].astype(BF16)
    col = 3 * len(DIL_GROUPS) * GROUP_WIDTH
    qm_ref[...] = (_dot(xb, w_ref[:, col:col + MEM_WIDTH]) * (HEAD_DIM ** -0.5)).astype(BF16)


def _a_projection(x2, w_in, tabs, B, S):
    n = x2.shape[0]
    tm = ROW_TILE
    per_seq = S // tm
    out_shapes, out_specs = [], []
    for _ in range(3):
        for _, dil in DIL_GROUPS:
            out_shapes.append(jax.ShapeDtypeStruct((B, dil, S // dil, GROUP_WIDTH), BF16))
            out_specs.append(pl.BlockSpec((1, dil, tm // dil, GROUP_WIDTH),
                                          lambda i: (i // per_seq, 0, i % per_seq, 0)))
    out_shapes.append(jax.ShapeDtypeStruct((n, MEM_WIDTH), BF16))
    out_specs.append(pl.BlockSpec((tm, MEM_WIDTH), lambda i: (i, 0)))
    tab_spec = pl.BlockSpec((tm, LANES), lambda i: (i % per_seq, 0))
    return pl.pallas_call(
        _a_proj_kernel,
        grid=(n // tm,),
        in_specs=[pl.BlockSpec((tm, D_MODEL), lambda i: (i, 0)),
                  pl.BlockSpec(w_in.shape, lambda i: (0, 0)),
                  tab_spec, tab_spec, tab_spec],
        out_specs=out_specs,
        out_shape=out_shapes,
        scratch_shapes=[pltpu.VMEM((GROUP_WIDTH // LANES, tm, LANES), F32)],
        compiler_params=_params(1),
        name="a_projection",
    )(x2, w_in, *tabs)


def _band_kernel(q_ref, k_ref, v_ref, o_ref, lse_ref, *, T):
    i = pl.program_id(1)
    start = pl.multiple_of(jnp.clip(i * BAND_Q - BAND_RADIUS, 0, T - BAND_KEYS), BAND_RADIUS)
    q = q_ref[0]
    k = k_ref[0, pl.ds(start, BAND_KEYS), :]
    v = v_ref[0, pl.ds(start, BAND_KEYS), :]
    qpos = i * BAND_Q + lax.broadcasted_iota(jnp.int32, (BAND_Q, BAND_KEYS), 0)
    kpos = start + lax.broadcasted_iota(jnp.int32, (BAND_Q, BAND_KEYS), 1)
    in_band = jnp.abs(qpos - kpos) <= BAND_RADIUS
    lane = lax.broadcasted_iota(jnp.int32, (BAND_Q, LANES), 1)
    for pair in range(GROUP_WIDTH // LANES):
        sl = slice(pair * LANES, (pair + 1) * LANES)
        qp, kp, vp = q[:, sl], k[:, sl], v[:, sl]
        o_pair = jnp.zeros((BAND_Q, LANES), F32)
        lse_pair = jnp.zeros((BAND_Q, LANES), F32)
        for hh in range(LANES // HEAD_DIM):
            own = (lane >= hh * HEAD_DIM) & (lane < (hh + 1) * HEAD_DIM)
            qh = jnp.where(own, qp, jnp.zeros_like(qp))
            s = jnp.where(in_band, _dot_nt(qh, kp), NEG_INF)
            m = jnp.max(s, axis=-1, keepdims=True)
            p = jnp.exp(s - m)
            l = jnp.sum(p, axis=-1, keepdims=True)
            o = _dot(p.astype(BF16), vp) / l
            o_pair = jnp.where(own, o, o_pair)
            lse_pair = jnp.where(own, m + jnp.log(l), lse_pair)
        o_ref[0, :, sl] = o_pair
        lse_ref[0, :, sl] = lse_pair


def _band_attention(q, k, v):
    NS, T, W = q.shape
    blk = pl.BlockSpec((1, BAND_Q, W), lambda s, i: (s, i, 0))
    seq = pl.BlockSpec((1, T, W), lambda s, i: (s, 0, 0))
    return pl.pallas_call(
        functools.partial(_band_kernel, T=T),
        grid=(NS, T // BAND_Q),
        in_specs=[blk, seq, seq],
        out_specs=[blk, blk],
        out_shape=[jax.ShapeDtypeStruct((NS, T, W), F32)] * 2,
        compiler_params=_params(2),
        name="band_attention",
    )(q, k, v)


def _a_merge_kernel(*refs):
    ng = len(DIL_GROUPS)
    o_refs, l_refs, out_ref = refs[:ng], refs[ng:2 * ng], refs[2 * ng]
    scr_o, scr_l = refs[2 * ng + 1], refs[2 * ng + 2]
    tm = out_ref.shape[0]
    os_, ls_ = [], []
    for g, (_, dil) in enumerate(DIL_GROUPS):
        if dil == 1:
            os_.append(o_refs[g][0, 0])
            ls_.append(l_refs[g][0, 0])
        else:
            nh = GROUP_WIDTH // LANES
            for r in range(dil):
                for j in range(nh):
                    sl = slice(j * LANES, (j + 1) * LANES)
                    scr_o[g * nh + j, pl.ds(r, tm // dil, stride=dil), :] = o_refs[g][0, r, :, sl]
                    scr_l[g * nh + j, pl.ds(r, tm // dil, stride=dil), :] = l_refs[g][0, r, :, sl]
            os_.append(jnp.concatenate([scr_o[g * nh + j] for j in range(nh)], axis=1))
            ls_.append(jnp.concatenate([scr_l[g * nh + j] for j in range(nh)], axis=1))
    m = functools.reduce(jnp.maximum, ls_)
    ws = [jnp.exp(l - m) for l in ls_]
    den = functools.reduce(jnp.add, ws)
    num = functools.reduce(jnp.add, [w * o for w, o in zip(ws, os_)])
    out_ref[...] = (num / den).astype(BF16)


def _a_merge(outs, lses, B, S):
    n = B * S
    tm = ROW_TILE
    per_seq = S // tm
    specs = [pl.BlockSpec((1, dil, tm // dil, GROUP_WIDTH), lambda i: (i // per_seq, 0, i % per_seq, 0))
             for _, dil in DIL_GROUPS]
    return pl.pallas_call(
        _a_merge_kernel,
        grid=(n // tm,),
        in_specs=specs + specs,
        out_specs=pl.BlockSpec((tm, GROUP_WIDTH), lambda i: (i, 0)),
        out_shape=jax.ShapeDtypeStruct((n, GROUP_WIDTH), BF16),
        scratch_shapes=[pltpu.VMEM((len(DIL_GROUPS) * GROUP_WIDTH // LANES, tm, LANES), F32)] * 2,
        compiler_params=_params(1),
        name="a_merge",
    )(*outs, *lses)


def _b_proj_kernel(x_ref, w_ref, qg_ref, kvg_ref, wq_ref, wk_ref, wv_ref, e_ref,
                   cq_ref, sqlo_ref, sqhi_ref, ck_ref, sklo_ref, skhi_ref,
                   q_out, k_out, v_out, qm_out):
    xb = x_ref[...].astype(BF16)
    proj = _dot(xb, w_ref[...])
    c_q = proj[:, :Q_LORA]
    qm = proj[:, Q_LORA:Q_LORA + MEM_WIDTH]
    c_kv = proj[:, Q_LORA + MEM_WIDTH:Q_LORA + MEM_WIDTH + KV_LORA]
    k_rope = proj[:, Q_LORA + MEM_WIDTH + KV_LORA:]
    qm_out[...] = (qm * (HEAD_DIM ** -0.5)).astype(BF16)

    qn = c_q * lax.rsqrt(jnp.mean(c_q * c_q, axis=-1, keepdims=True) + RMS_EPS) * qg_ref[...]
    kvn = c_kv * lax.rsqrt(jnp.mean(c_kv * c_kv, axis=-1, keepdims=True) + RMS_EPS) * kvg_ref[...]
    qn, kvn = qn.astype(BF16), kvn.astype(BF16)

    cq, sqlo, sqhi = cq_ref[...], sqlo_ref[...], sqhi_ref[...]
    scale = (MLA_NOPE + MLA_ROPE) ** -0.5 * LOG2E
    for h in range(MLA_HEADS):
        sl = slice(h * MLA_QK_PAD, (h + 1) * MLA_QK_PAD)
        qh = _rope_lanes(_dot(qn, wq_ref[:, sl]), cq, sqlo, sqhi, MLA_ROPE // 2)
        q_out[:, sl] = (qh * scale).astype(BF16)

    k_pe = _rope_lanes(k_rope, ck_ref[...], sklo_ref[...], skhi_ref[...], MLA_ROPE // 2).astype(BF16)
    k_out[...] = (_dot(kvn, wk_ref[...]) + _dot(k_pe, e_ref[...])).astype(BF16)
    v_out[...] = _dot_nt(wv_ref[...], kvn).astype(BF16)


def _b_projection(x2, w_in, q_g, kv_g, wq, wk, wv, e_mat, tabs, S):
    n = x2.shape[0]
    tm = ROW_TILE
    per_seq = S // tm
    full = lambda a: pl.BlockSpec(a.shape, lambda i: (0,) * a.ndim)
    tab_spec = pl.BlockSpec((tm, LANES), lambda i: (i % per_seq, 0))
    row = lambda w: pl.BlockSpec((tm, w), lambda i: (i, 0))
    hw = MLA_HEADS * MLA_QK_PAD
    return pl.pallas_call(
        _b_proj_kernel,
        grid=(n // tm,),
        in_specs=[row(D_MODEL), full(w_in), full(q_g), full(kv_g), full(wq), full(wk), full(wv), full(e_mat)]
                 + [tab_spec] * 6,
        out_specs=[row(hw), row(hw), pl.BlockSpec((MLA_HEADS * MLA_V, tm), lambda i: (0, i)), row(MEM_WIDTH)],
        out_shape=[jax.ShapeDtypeStruct((n, hw), BF16), jax.ShapeDtypeStruct((n, hw), BF16),
                   jax.ShapeDtypeStruct((MLA_HEADS * MLA_V, n), BF16),
                   jax.ShapeDtypeStruct((n, MEM_WIDTH), BF16)],
        compiler_params=_params(1),
        name="b_projection",
    )(x2, w_in, q_g, kv_g, wq, wk, wv, e_mat, *tabs)


def _mla_kernel(q_ref, k_ref, vt_ref, o_ref, s_scr, *, S):
    tq = q_ref.shape[0]
    tk = MLA_TK
    outs = []
    for hh in range(LANES // MLA_V):
        sl = slice(hh * MLA_QK_PAD, (hh + 1) * MLA_QK_PAD)
        q = q_ref[:, sl]
        mx8 = jnp.full((8, tq), NEG_INF, F32)
        for c in range(S // tk):
            s = _dot_nt(k_ref[c * tk:(c + 1) * tk, sl], q)
            s_scr[hh, c * tk:(c + 1) * tk, :] = s
            mx8 = jnp.maximum(mx8, jnp.max(s.reshape(tk // 8, 8, tq), axis=0))
        m = jnp.max(mx8, axis=0, keepdims=True)
        l8 = jnp.zeros((8, tq), F32)
        acc = jnp.zeros((MLA_V, tq), F32)
        for c in range(S // tk):
            p = jnp.exp2(s_scr[hh, c * tk:(c + 1) * tk, :] - m)
            l8 = l8 + jnp.sum(p.reshape(tk // 8, 8, tq), axis=0)
            acc = acc + _dot(vt_ref[hh * MLA_V:(hh + 1) * MLA_V, c * tk:(c + 1) * tk], p.astype(BF16))
        outs.append(acc / jnp.sum(l8, axis=0, keepdims=True))
    o_ref[...] = jnp.concatenate(outs, axis=0).T.astype(BF16)


def _mla_attention(q, k, vt, B, S):
    n = B * S
    per_seq = S // MLA_TQ
    pairs = MLA_HEADS // 2
    return pl.pallas_call(
        functools.partial(_mla_kernel, S=S),
        grid=(B, pairs, per_seq),
        in_specs=[pl.BlockSpec((MLA_TQ, 2 * MLA_QK_PAD), lambda b, p, i: (b * per_seq + i, p)),
                  pl.BlockSpec((S, 2 * MLA_QK_PAD), lambda b, p, i: (b, p)),
                  pl.BlockSpec((2 * MLA_V, S), lambda b, p, i: (p, b))],
        out_specs=pl.BlockSpec((MLA_TQ, 2 * MLA_V), lambda b, p, i: (b * per_seq + i, p)),
        out_shape=jax.ShapeDtypeStruct((n, MLA_HEADS * MLA_V), BF16),
        scratch_shapes=[pltpu.VMEM((2, S, MLA_TQ), F32)],
        compiler_params=_params(3),
        name="mla_attention",
    )(q, k, vt)


def _cast_matmul_kernel(x_ref, w_ref, o_ref):
    o_ref[...] = _dot(x_ref[...].astype(BF16), w_ref[...]).astype(o_ref.dtype)


def _mem_kv(mem2, w):
    n, d = mem2.shape
    tm = min(ROW_TILE, n)
    return pl.pallas_call(
        _cast_matmul_kernel,
        grid=(n // tm,),
        in_specs=[pl.BlockSpec((tm, d), lambda i: (i, 0)), pl.BlockSpec(w.shape, lambda i: (0, 0))],
        out_specs=pl.BlockSpec((tm, w.shape[1]), lambda i: (i, 0)),
        out_shape=jax.ShapeDtypeStruct((n, w.shape[1]), BF16),
        compiler_params=_params(1),
        name="mem_kv",
    )(mem2, w)


def _mem_attn_kernel(q_ref, kv_ref, o_ref):
    tq = q_ref.shape[0]
    lane = lax.broadcasted_iota(jnp.int32, (tq, LANES), 1)
    for pair in range(MEM_WIDTH // LANES):
        sl = slice(pair * LANES, (pair + 1) * LANES)
        qp = q_ref[:, sl]
        kp = kv_ref[:, sl]
        vp = kv_ref[:, MEM_WIDTH + pair * LANES:MEM_WIDTH + (pair + 1) * LANES]
        o_pair = jnp.zeros((tq, LANES), F32)
        for hh in range(LANES // HEAD_DIM):
            own = (lane >= hh * HEAD_DIM) & (lane < (hh + 1) * HEAD_DIM)
            s = _dot_nt(jnp.where(own, qp, jnp.zeros_like(qp)), kp)
            m = jnp.max(s, axis=-1, keepdims=True)
            p = jnp.exp(s - m)
            l = jnp.sum(p, axis=-1, keepdims=True)
            o_pair = jnp.where(own, _dot(p.astype(BF16), vp) / l, o_pair)
        o_ref[:, sl] = o_pair.astype(BF16)


def _mem_attention(qm, kv, B, S, M):
    n = B * S
    per_seq = S // MEM_TQ
    return pl.pallas_call(
        _mem_attn_kernel,
        grid=(n // MEM_TQ,),
        in_specs=[pl.BlockSpec((MEM_TQ, MEM_WIDTH), lambda i: (i, 0)),
                  pl.BlockSpec((M, 2 * MEM_WIDTH), lambda i: (i // per_seq, 0))],
        out_specs=pl.BlockSpec((MEM_TQ, MEM_WIDTH), lambda i: (i, 0)),
        out_shape=jax.ShapeDtypeStruct((n, MEM_WIDTH), BF16),
        compiler_params=_params(1),
        name="mem_attention",
    )(qm, kv)


def _pack_bf16_pairs(x):
    w = x.shape[1] // 2
    hi = lax.bitcast_convert_type(x[:, :w].astype(BF16).astype(F32), jnp.uint32)
    lo = lax.bitcast_convert_type(x[:, w:].astype(BF16).astype(F32), jnp.uint32)
    return hi | (lo >> 16)


def _unpack_bf16_pairs(w):
    hi = lax.bitcast_convert_type(w & jnp.uint32(0xFFFF0000), F32)
    lo = lax.bitcast_convert_type(w << 16, F32)
    return hi, lo


def _route(logits, tri, run):
    lane = lax.broadcasted_iota(jnp.int32, logits.shape, 1).astype(F32)
    big = jnp.float32(1e9)
    is_grp = lane < N_GROUPS
    gl = jnp.where(is_grp, logits, NEG_INF)
    gmax = jnp.max(gl, axis=-1, keepdims=True)
    gidx = jnp.min(jnp.where(gl >= gmax, lane, big), axis=-1, keepdims=True)
    gsum = jnp.sum(jnp.where(is_grp, jnp.exp(gl - gmax), 0.0), axis=-1, keepdims=True)
    g_gate = 1.0 / gsum
    lo = N_GROUPS + EXPERTS_PER_GROUP * gidx
    el = jnp.where((lane >= lo) & (lane < lo + EXPERTS_PER_GROUP), logits, NEG_INF)
    v1 = jnp.max(el, axis=-1, keepdims=True)
    i1 = jnp.min(jnp.where(el >= v1, lane, big), axis=-1, keepdims=True)
    el2 = jnp.where(lane == i1, NEG_INF, el)
    v2 = jnp.max(el2, axis=-1, keepdims=True)
    i2 = jnp.min(jnp.where(el2 >= v2, lane, big), axis=-1, keepdims=True)
    e2 = jnp.exp(v2 - v1)
    den = 1.0 + e2
    gate1 = g_gate / den
    gate2 = g_gate * e2 / den
    e1 = i1 - N_GROUPS
    e2 = i2 - N_GROUPS
    oh1 = lane == e1
    oh2 = lane == e2
    member = jnp.where(oh1 | oh2, 1.0, 0.0)
    before = _dot(tri, member.astype(BF16)) + run
    rank1 = jnp.sum(jnp.where(oh1, before, 0.0), axis=-1, keepdims=True)
    rank2 = jnp.sum(jnp.where(oh2, before, 0.0), axis=-1, keepdims=True)
    slab = jnp.zeros_like(logits)
    for k, col in enumerate((e1, e2, gate1, gate2, rank1, rank2)):
        slab = jnp.where(lane == k, col, slab)
    return slab, jnp.sum(member, axis=0, keepdims=True)


def _post_kernel(x_ref, mix_ref, mo_ref, w1_ref, w2_ref, g_ref, b_ref, wr_ref, br_ref, tri_ref,
                 xn_ref, xw_ref, route_ref, cnt_ref, run_ref):
    @pl.when(pl.program_id(0) == 0)
    def _():
        run_ref[...] = jnp.zeros_like(run_ref)

    y = _dot(mix_ref[...], w1_ref[...]) + _dot(mo_ref[...], w2_ref[...])
    xn = _layer_norm(ALPHA * x_ref[...] + y, g_ref[...], b_ref[...])
    xn_ref[...] = xn
    xw_ref[...] = _pack_bf16_pairs(xn)
    slab, counts = _route(_dot(xn.astype(BF16), wr_ref[...]) + br_ref[...], tri_ref[...], run_ref[0:1, :])
    route_ref[...] = slab
    run_ref[...] = run_ref[...] + counts
    cnt_ref[...] = run_ref[...]


def _post_mixer(x2, mix, mo, w1, w2, g, b, wr, br):
    n = x2.shape[0]
    tm = ROW_TILE
    full = lambda a: pl.BlockSpec(a.shape, lambda i: (0,) * a.ndim)
    row = lambda w: pl.BlockSpec((tm, w), lambda i: (i, 0))
    idx = jnp.arange(tm)
    tri = (idx[:, None] > idx[None, :]).astype(BF16)
    return pl.pallas_call(
        _post_kernel,
        grid=(n // tm,),
        in_specs=[row(D_MODEL), row(mix.shape[1]), row(mo.shape[1]), full(w1), full(w2), full(g), full(b),
                  full(wr), full(br), full(tri)],
        out_specs=[row(D_MODEL), row(D_MODEL // 2), row(LANES), pl.BlockSpec((8, LANES), lambda i: (0, 0))],
        out_shape=[jax.ShapeDtypeStruct((n, D_MODEL), F32), jax.ShapeDtypeStruct((n, D_MODEL // 2), jnp.uint32),
                   jax.ShapeDtypeStruct((n, LANES), F32), jax.ShapeDtypeStruct((8, LANES), F32)],
        scratch_shapes=[pltpu.VMEM((8, LANES), F32)],
        compiler_params=_params(1),
        name="post_mixer",
    )(x2, mix, mo, w1, w2, g, b, wr, br, tri)


def _row_copy(src, src_row, dst, dst_row, sem):
    return pltpu.make_async_copy(src.at[pl.ds(src_row, 1), :], dst.at[pl.ds(dst_row, 1), :], sem)


def _dispatch_kernel(dest_ref, xw_ref, xs_in_ref, xs_ref, sem):
    del xs_in_ref
    tm = xw_ref.shape[0]

    def issue(r, carry):
        for k in range(TOP_K):
            _row_copy(xw_ref, r, xs_ref, dest_ref[0, k, r], sem).start()
        return carry

    lax.fori_loop(0, tm, issue, 0, unroll=8)

    def drain(r, carry):
        for k in range(TOP_K):
            _row_copy(xw_ref, 0, xs_ref, 0, sem).wait()
        return carry

    lax.fori_loop(0, tm, drain, 0, unroll=8)


def _dispatch(xw, dest, n_slots):
    n, w = xw.shape
    tm = ROW_TILE
    return pl.pallas_call(
        _dispatch_kernel,
        grid=(n // tm,),
        in_specs=[pl.BlockSpec((1, TOP_K, tm), lambda i: (i, 0, 0), memory_space=pltpu.SMEM),
                  pl.BlockSpec((tm, w), lambda i: (i, 0)),
                  pl.BlockSpec(memory_space=pl.ANY)],
        out_specs=pl.BlockSpec(memory_space=pl.ANY),
        out_shape=jax.ShapeDtypeStruct((n_slots, w), jnp.uint32),
        scratch_shapes=[pltpu.SemaphoreType.DMA(())],
        input_output_aliases={2: 0},
        compiler_params=_params(1),
        name="moe_dispatch",
    )(dest, xw, jnp.zeros((n_slots, w), jnp.uint32))


def _expert_kernel(blk_e_ref, n_used_ref, xs_ref, wg_ref, wu_ref, wd_ref, ys_ref, wg_s, wu_s, wd_s):
    i = pl.program_id(0)
    half = D_MODEL // 2

    @pl.when((i == 0) | (blk_e_ref[i] != blk_e_ref[jnp.maximum(i - 1, 0)]))
    def _():
        wg_s[...] = wg_ref[0].astype(BF16)
        wu_s[...] = wu_ref[0].astype(BF16)
        wd_s[...] = wd_ref[0].astype(BF16)

    @pl.when(i < n_used_ref[0])
    def _():
        hi, lo = _unpack_bf16_pairs(xs_ref[...])
        hi, lo = hi.astype(BF16), lo.astype(BF16)
        gate = _dot(hi, wg_s[:half, :]) + _dot(lo, wg_s[half:, :])
        up = _dot(hi, wu_s[:half, :]) + _dot(lo, wu_s[half:, :])
        h = (gate * jax.nn.sigmoid(gate) * up).astype(BF16)
        ys_ref[...] = _pack_bf16_pairs(_dot(h, wd_s[...]))

    @pl.when(i >= n_used_ref[0])
    def _():
        ys_ref[...] = jnp.zeros_like(ys_ref)


def _experts(xs, blk_expert, n_used, wg, wu, wd):
    n_slots, w = xs.shape
    tm = MOE_TILE
    grid_spec = pltpu.PrefetchScalarGridSpec(
        num_scalar_prefetch=2,
        grid=(n_slots // tm,),
        in_specs=[pl.BlockSpec((tm, w), lambda i, be, nu: (i, 0)),
                  pl.BlockSpec((1, D_MODEL, D_EXPERT), lambda i, be, nu: (be[i], 0, 0)),
                  pl.BlockSpec((1, D_MODEL, D_EXPERT), lambda i, be, nu: (be[i], 0, 0)),
                  pl.BlockSpec((1, D_EXPERT, D_MODEL), lambda i, be, nu: (be[i], 0, 0))],
        out_specs=pl.BlockSpec((tm, w), lambda i, be, nu: (i, 0)),
        scratch_shapes=[pltpu.VMEM((D_MODEL, D_EXPERT), BF16), pltpu.VMEM((D_MODEL, D_EXPERT), BF16),
                        pltpu.VMEM((D_EXPERT, D_MODEL), BF16)],
    )
    return pl.pallas_call(
        _expert_kernel,
        grid_spec=grid_spec,
        out_shape=jax.ShapeDtypeStruct((n_slots, w), jnp.uint32),
        compiler_params=_params(1),
        name="experts",
    )(blk_expert, n_used, xs, wg, wu, wd)


def _ffn_ln_kernel(dest_ref, xn_ref, route_ref, g_ref, b_ref, ys_ref, o_ref, ybuf, sem):
    tm = xn_ref.shape[0]

    def issue(r, carry):
        for k in range(TOP_K):
            _row_copy(ys_ref, dest_ref[0, k, r], ybuf.at[k], r, sem).start()
        return carry

    lax.fori_loop(0, tm, issue, 0, unroll=8)

    def drain(r, carry):
        for k in range(TOP_K):
            _row_copy(ys_ref, 0, ybuf.at[k], 0, sem).wait()
        return carry

    lax.fori_loop(0, tm, drain, 0, unroll=8)

    route = route_ref[...]
    g1 = route[:, 2:3]
    g2 = route[:, 3:4]
    hi1, lo1 = _unpack_bf16_pairs(ybuf[0])
    hi2, lo2 = _unpack_bf16_pairs(ybuf[1])
    f = jnp.concatenate([g1 * hi1 + g2 * hi2, g1 * lo1 + g2 * lo2], axis=1)
    o_ref[...] = _layer_norm(ALPHA * xn_ref[...] + f, g_ref[...], b_ref[...])


def _ffn_ln(xn, ys, dest, route, g, b):
    n = xn.shape[0]
    tm = ROW_TILE
    full = lambda a: pl.BlockSpec(a.shape, lambda i: (0,) * a.ndim)
    row = lambda w: pl.BlockSpec((tm, w), lambda i: (i, 0))
    return pl.pallas_call(
        _ffn_ln_kernel,
        grid=(n // tm,),
        in_specs=[pl.BlockSpec((1, TOP_K, tm), lambda i: (i, 0, 0), memory_space=pltpu.SMEM),
                  row(D_MODEL), row(LANES), full(g), full(b), pl.BlockSpec(memory_space=pl.ANY)],
        out_specs=row(D_MODEL),
        out_shape=jax.ShapeDtypeStruct((n, D_MODEL), F32),
        scratch_shapes=[pltpu.VMEM((TOP_K, tm, ys.shape[1]), jnp.uint32), pltpu.SemaphoreType.DMA(())],
        compiler_params=_params(1),
        name="ffn_ln",
    )(dest, xn, route, g, b, ys)


def _moe(xn, xw, route, counts, wg, wu, wd, ln_g, ln_b):
    n = xn.shape[0]
    tm = MOE_TILE
    n_blocks = n * TOP_K // tm + N_EXPERTS
    cnt = counts[0, :N_EXPERTS].astype(jnp.int32)
    padded = (cnt + tm - 1) // tm * tm
    pends = jnp.cumsum(padded)
    pstarts = pends - padded
    eid = route[:, :TOP_K].astype(jnp.int32)
    rank = route[:, 4:4 + TOP_K].astype(jnp.int32)
    dest = jnp.take(pstarts, eid, mode="clip") + rank
    dest = dest.reshape(n // ROW_TILE, ROW_TILE, TOP_K).transpose(0, 2, 1)
    blk_start = jnp.arange(n_blocks, dtype=jnp.int32) * tm
    blk_expert = jnp.minimum(jnp.sum(pends[None, :] <= blk_start[:, None], axis=1), N_EXPERTS - 1).astype(jnp.int32)
    n_used = (pends[-1] // tm).astype(jnp.int32).reshape(1)
    xs = _dispatch(xw, dest, n_blocks * tm)
    ys = _experts(xs, blk_expert, n_used, wg, wu, wd)
    return _ffn_ln(xn, ys, dest, route, ln_g, ln_b)


def _rope_tables(S, dim):
    inv = ROPE_THETA ** (-jnp.arange(0, dim, 2, dtype=F32) / dim)
    ang = jnp.arange(S, dtype=F32)[:, None] * inv[None, :]
    return jnp.cos(ang), jnp.sin(ang)


def _lane_tables(cos, sin, first, period):
    S, half = cos.shape
    z = jnp.zeros((S, half), F32)

    def build(rot_lo, rot_hi, fill):
        unit = jnp.concatenate([jnp.full((S, first), fill, F32), rot_lo, rot_hi,
                                jnp.full((S, period - first - 2 * half), fill, F32)], axis=1)
        return jnp.tile(unit, (1, LANES // period))

    return build(cos, cos, 1.0), build(-sin, z, 0.0), build(z, sin, 0.0)


def _router_weights(w_grp, b_grp, w_exp, b_exp):
    pad = LANES - N_GROUPS - N_EXPERTS
    wr = jnp.concatenate([w_grp, w_exp, jnp.zeros((D_MODEL, pad), F32)], axis=1).astype(BF16)
    br = jnp.concatenate([b_grp, b_exp, jnp.zeros((pad,), F32)]).reshape(1, LANES)
    return wr, br


def kernel(x, mem, a_w_in, a_w_out, b_w_in, b_q_norm, b_w_qb, b_kv_norm, b_w_kvb, b_w_out, mem_w_kv, ln_mix_g, ln_mix_b, ln_ffn_g, ln_ffn_b, router_w_grp, router_b_grp, router_w_exp, router_b_exp, exp_w_gate, exp_w_up, exp_w_down):
    B, S, D = x.shape
    M = mem.shape[1]
    n = B * S
    x2 = x.reshape(n, D)
    mem2 = mem.reshape(B * M, D)
    cos_a, sin_a = _rope_tables(S, ROT_DIM)
    cos_b, sin_b = _rope_tables(S, MLA_ROPE)
    tabs_a = _lane_tables(cos_a, sin_a, 0, HEAD_DIM)
    tabs_bq = _lane_tables(cos_b, sin_b, MLA_NOPE, MLA_QK_PAD)
    tabs_bk = _lane_tables(cos_b, sin_b, 0, LANES)
    row = lambda v: v.reshape(1, -1)

    for i in range(DEPTH):
        j = i // 2
        if i % 2 == 0:
            outs = _a_projection(x2, a_w_in[j].astype(BF16), tabs_a, B, S)
            ng = len(DIL_GROUPS)
            o_g, lse_g = [], []
            for g, (_, dil) in enumerate(DIL_GROUPS):
                T = S // dil
                sub = lambda t: t.reshape(B * dil, T, GROUP_WIDTH)
                o, lse = _band_attention(sub(outs[g]), sub(outs[ng + g]), sub(outs[2 * ng + g]))
                o_g.append(o.reshape(B, dil, T, GROUP_WIDTH))
                lse_g.append(lse.reshape(B, dil, T, GROUP_WIDTH))
            qm = outs[-1]
            mix = _a_merge(o_g, lse_g, B, S)
            w_out = a_w_out[j]
        else:
            w_in = b_w_in[j]
            c_q, c_kv, k_rope, w_qm = jnp.split(w_in, [Q_LORA, Q_LORA + KV_LORA, Q_LORA + KV_LORA + MLA_ROPE], axis=1)
            w_in_p = jnp.concatenate([c_q, w_qm, c_kv, k_rope, jnp.zeros((D, LANES - MLA_ROPE), F32)], axis=1)
            wq = jnp.pad(b_w_qb[j].reshape(Q_LORA, MLA_HEADS, MLA_NOPE + MLA_ROPE),
                         ((0, 0), (0, 0), (0, MLA_QK_PAD - MLA_NOPE - MLA_ROPE))).reshape(Q_LORA, -1)
            wkv = b_w_kvb[j].reshape(KV_LORA, MLA_HEADS, MLA_NOPE + MLA_V)
            wk = jnp.pad(wkv[:, :, :MLA_NOPE], ((0, 0), (0, 0), (0, MLA_QK_PAD - MLA_NOPE))).reshape(KV_LORA, -1)
            wv = wkv[:, :, MLA_NOPE:].reshape(KV_LORA, -1).T
            rows = jnp.arange(LANES)[:, None]
            cols = jnp.arange(MLA_HEADS * MLA_QK_PAD)[None, :]
            e_mat = ((rows < MLA_ROPE) & (cols % MLA_QK_PAD == rows + MLA_NOPE)).astype(BF16)
            q, k, v, qm = _b_projection(x2, w_in_p.astype(BF16), row(b_q_norm[j]), row(b_kv_norm[j]),
                                        wq.astype(BF16), wk.astype(BF16), wv.astype(BF16), e_mat,
                                        tabs_bq + tabs_bk, S)
            mix = _mla_attention(q, k, v, B, S)
            w_out = b_w_out[j]
        kv_m = _mem_kv(mem2, mem_w_kv[i].astype(BF16))
        mo = _mem_attention(qm, kv_m, B, S, M)
        wm = mix.shape[1]
        wr, br = _router_weights(router_w_grp[i], router_b_grp[i], router_w_exp[i], router_b_exp[i])
        xn, xw, route, counts = _post_mixer(x2, mix, mo, w_out[:wm].astype(BF16), w_out[wm:].astype(BF16),
                                            row(ln_mix_g[i]), row(ln_mix_b[i]), wr, br)
        x2 = _moe(xn, xw, route, counts, exp_w_gate[i], exp_w_up[i], exp_w_down[i],
                  row(ln_ffn_g[i]), row(ln_ffn_b[i]))
    return x2.reshape(B, S, D)
```

```python
import functools

import jax
import jax.numpy as jnp
from jax import lax
from jax.experimental import pallas as pl
from jax.experimental.pallas import tpu as pltpu

D_MODEL = 1024
HEAD_DIM = 64
ROPE_THETA = 500000.0
ROT_DIM = HEAD_DIM // 4

DIL_GROUPS = ((128, 1), (512, 4), (2048, 16))
DIL_HEADS = 4
GROUP_WIDTH = DIL_HEADS * HEAD_DIM
BAND_RADIUS = 64
BAND_Q = 128
BAND_KEYS = BAND_Q + 2 * BAND_RADIUS
BAND_BLOCKS_PER_STEP = 4

MEM_HEADS = 4
MEM_WIDTH = MEM_HEADS * HEAD_DIM

MLA_HEADS = 8
MLA_NOPE = 64
MLA_ROPE = 32
MLA_V = 64
Q_LORA = 256
KV_LORA = 128
MLA_QK_PAD = 128

N_GROUPS = 4
EXPERTS_PER_GROUP = 8
N_EXPERTS = N_GROUPS * EXPERTS_PER_GROUP
TOP_K = 2
D_EXPERT = 512

LN_EPS = 1e-5
RMS_EPS = 1e-6
NEG_INF = -1e30
DEPTH = 2
ALPHA = (2 * DEPTH) ** 0.25
LOG2E = 1.4426950408889634

LANES = 128
ROW_TILE = 512
MOE_TILE = 512
MLA_TQ = 256
MLA_TK = 1024
MEM_TQ = 512
VMEM_LIMIT = 48 * 1024 * 1024

BF16 = jnp.bfloat16
F32 = jnp.float32


def _params(n_axes):
    return pltpu.CompilerParams(dimension_semantics=("arbitrary",) * n_axes, vmem_limit_bytes=VMEM_LIMIT)


def _dot(a, b):
    return jnp.dot(a, b, preferred_element_type=F32)


def _dot_nt(a, b):
    return lax.dot_general(a, b, (((1,), (1,)), ((), ())), preferred_element_type=F32)


def _rope_lanes(t, c, s_lo, s_hi, half):
    return t * c + pltpu.roll(t, LANES - half, 1) * s_lo + pltpu.roll(t, half, 1) * s_hi


def _layer_norm(z, g, b):
    mu = jnp.mean(z, axis=-1, keepdims=True)
    zc = z - mu
    var = jnp.mean(zc * zc, axis=-1, keepdims=True)
    return zc * lax.rsqrt(var + LN_EPS) * g + b


def _a_proj_kernel(x_ref, w_ref, c_ref, slo_ref, shi_ref, *rest):
    outs, scr = rest[:-1], rest[-1]
    qm_ref = outs[-1]
    xb = x_ref[...].astype(BF16)
    c, slo, shi = c_ref[...], slo_ref[...], shi_ref[...]
    tm = xb.shape[0]
    for part in range(3):
        for g, (_, dil) in enumerate(DIL_GROUPS):
            col = (part * len(DIL_GROUPS) + g) * GROUP_WIDTH
            res = _dot(xb, w_ref[:, col:col + GROUP_WIDTH])
            o_ref = outs[part * len(DIL_GROUPS) + g]
            for j in range(GROUP_WIDTH // LANES):
                sl = slice(j * LANES, (j + 1) * LANES)
                half = res[:, sl]
                if part < 2:
                    half = _rope_lanes(half, c, slo, shi, ROT_DIM // 2)
                if part == 0:
                    half = half * (HEAD_DIM ** -0.5)
                if dil == 1:
                    o_ref[0, 0, :, sl] = half.astype(BF16)
                else:
                    scr[j] = half
                    for r in range(dil):
                        o_ref[0, r, :, sl] = scr[j, pl.ds(r, tm // dil, stride=dil), :].astype(BF16)
    col = 3 * len(DIL_GROUPS) * GROUP_WIDTH
    qm_ref[...] = (_dot(xb, w_ref[:, col:col + MEM_WIDTH]) * (HEAD_DIM ** -0.5)).astype(BF16)


def _a_projection(x2, w_in, tabs, B, S):
    n = x2.shape[0]
    tm = ROW_TILE
    per_seq = S // tm
    out_shapes, out_specs = [], []
    for _ in range(3):
        for _, dil in DIL_GROUPS:
            out_shapes.append(jax.ShapeDtypeStruct((B, dil, S // dil, GROUP_WIDTH), BF16))
            out_specs.append(pl.BlockSpec((1, dil, tm // dil, GROUP_WIDTH),
                                          lambda i: (i // per_seq, 0, i % per_seq, 0)))
    out_shapes.append(jax.ShapeDtypeStruct((n, MEM_WIDTH), BF16))
    out_specs.append(pl.BlockSpec((tm, MEM_WIDTH), lambda i: (i, 0)))
    tab_spec = pl.BlockSpec((tm, LANES), lambda i: (i % per_seq, 0))
    return pl.pallas_call(
        _a_proj_kernel,
        grid=(n // tm,),
        in_specs=[pl.BlockSpec((tm, D_MODEL), lambda i: (i, 0)),
                  pl.BlockSpec(w_in.shape, lambda i: (0, 0)),
                  tab_spec, tab_spec, tab_spec],
        out_specs=out_specs,
        out_shape=out_shapes,
        scratch_shapes=[pltpu.VMEM((GROUP_WIDTH // LANES, tm, LANES), F32)],
        compiler_params=_params(1),
        name="a_projection",
    )(x2, w_in, *tabs)


def _band_block(q, k_ref, v_ref, blk, T):
    start = pl.multiple_of(jnp.clip(blk * BAND_Q - BAND_RADIUS, 0, T - BAND_KEYS), BAND_RADIUS)
    k = k_ref[0, pl.ds(start, BAND_KEYS), :]
    v = v_ref[0, pl.ds(start, BAND_KEYS), :]
    qpos = blk * BAND_Q + lax.broadcasted_iota(jnp.int32, (BAND_Q, BAND_KEYS), 0)
    kpos = start + lax.broadcasted_iota(jnp.int32, (BAND_Q, BAND_KEYS), 1)
    in_band = jnp.abs(qpos - kpos) <= BAND_RADIUS
    lane = lax.broadcasted_iota(jnp.int32, (BAND_Q, LANES), 1)
    outs, lses = [], []
    for pair in range(GROUP_WIDTH // LANES):
        sl = slice(pair * LANES, (pair + 1) * LANES)
        qp, kp, vp = q[:, sl], k[:, sl], v[:, sl]
        o_pair = jnp.zeros((BAND_Q, LANES), F32)
        lse_pair = jnp.zeros((BAND_Q, LANES), F32)
        for hh in range(LANES // HEAD_DIM):
            own = (lane >= hh * HEAD_DIM) & (lane < (hh + 1) * HEAD_DIM)
            qh = jnp.where(own, qp, jnp.zeros_like(qp))
            s = jnp.where(in_band, _dot_nt(qh, kp), NEG_INF)
            m = jnp.max(s, axis=-1, keepdims=True)
            p = jnp.exp(s - m)
            l = jnp.sum(p, axis=-1, keepdims=True)
            o = _dot(p.astype(BF16), vp) / l
            o_pair = jnp.where(own, o, o_pair)
            lse_pair = jnp.where(own, m + jnp.log(l), lse_pair)
        outs.append(o_pair)
        lses.append(lse_pair)
    return jnp.concatenate(outs, axis=1), jnp.concatenate(lses, axis=1)


def _band_kernel(q_ref, k_ref, v_ref, o_ref, lse_ref, *, T, blocks):
    i = pl.program_id(1)
    for j in range(blocks):
        rows = slice(j * BAND_Q, (j + 1) * BAND_Q)
        o, lse = _band_block(q_ref[0, rows, :], k_ref, v_ref, i * blocks + j, T)
        o_ref[0, rows, :] = o
        lse_ref[0, rows, :] = lse


def _band_attention(q, k, v):
    NS, T, W = q.shape
    blocks = min(BAND_BLOCKS_PER_STEP, T // BAND_Q)
    blk = pl.BlockSpec((1, BAND_Q * blocks, W), lambda s, i: (s, i, 0))
    seq = pl.BlockSpec((1, T, W), lambda s, i: (s, 0, 0))
    return pl.pallas_call(
        functools.partial(_band_kernel, T=T, blocks=blocks),
        grid=(NS, T // (BAND_Q * blocks)),
        in_specs=[blk, seq, seq],
        out_specs=[blk, blk],
        out_shape=[jax.ShapeDtypeStruct((NS, T, W), F32)] * 2,
        compiler_params=_params(2),
        name="band_attention",
    )(q, k, v)


def _a_merge_kernel(*refs):
    ng = len(DIL_GROUPS)
    o_refs, l_refs, out_ref = refs[:ng], refs[ng:2 * ng], refs[2 * ng]
    scr_o, scr_l = refs[2 * ng + 1], refs[2 * ng + 2]
    tm = out_ref.shape[0]
    os_, ls_ = [], []
    for g, (_, dil) in enumerate(DIL_GROUPS):
        if dil == 1:
            os_.append(o_refs[g][0, 0])
            ls_.append(l_refs[g][0, 0])
        else:
            nh = GROUP_WIDTH // LANES
            for r in range(dil):
                for j in range(nh):
                    sl = slice(j * LANES, (j + 1) * LANES)
                    scr_o[g * nh + j, pl.ds(r, tm // dil, stride=dil), :] = o_refs[g][0, r, :, sl]
                    scr_l[g * nh + j, pl.ds(r, tm // dil, stride=dil), :] = l_refs[g][0, r, :, sl]
            os_.append(jnp.concatenate([scr_o[g * nh + j] for j in range(nh)], axis=1))
            ls_.append(jnp.concatenate([scr_l[g * nh + j] for j in range(nh)], axis=1))
    m = functools.reduce(jnp.maximum, ls_)
    ws = [jnp.exp(l - m) for l in ls_]
    den = functools.reduce(jnp.add, ws)
    num = functools.reduce(jnp.add, [w * o for w, o in zip(ws, os_)])
    out_ref[...] = (num / den).astype(BF16)


def _a_merge(outs, lses, B, S):
    n = B * S
    tm = ROW_TILE
    per_seq = S // tm
    specs = [pl.BlockSpec((1, dil, tm // dil, GROUP_WIDTH), lambda i: (i // per_seq, 0, i % per_seq, 0))
             for _, dil in DIL_GROUPS]
    return pl.pallas_call(
        _a_merge_kernel,
        grid=(n // tm,),
        in_specs=specs + specs,
        out_specs=pl.BlockSpec((tm, GROUP_WIDTH), lambda i: (i, 0)),
        out_shape=jax.ShapeDtypeStruct((n, GROUP_WIDTH), BF16),
        scratch_shapes=[pltpu.VMEM((len(DIL_GROUPS) * GROUP_WIDTH // LANES, tm, LANES), F32)] * 2,
        compiler_params=_params(1),
        name="a_merge",
    )(*outs, *lses)


def _b_proj_kernel(x_ref, w_ref, qg_ref, kvg_ref, wq_ref, wk_ref, wv_ref, e_ref,
                   cq_ref, sqlo_ref, sqhi_ref, ck_ref, sklo_ref, skhi_ref,
                   q_out, k_out, v_out, qm_out):
    xb = x_ref[...].astype(BF16)
    proj = _dot(xb, w_ref[...])
    c_q = proj[:, :Q_LORA]
    qm = proj[:, Q_LORA:Q_LORA + MEM_WIDTH]
    c_kv = proj[:, Q_LORA + MEM_WIDTH:Q_LORA + MEM_WIDTH + KV_LORA]
    k_rope = proj[:, Q_LORA + MEM_WIDTH + KV_LORA:]
    qm_out[...] = (qm * (HEAD_DIM ** -0.5)).astype(BF16)

    qn = c_q * lax.rsqrt(jnp.mean(c_q * c_q, axis=-1, keepdims=True) + RMS_EPS) * qg_ref[...]
    kvn = c_kv * lax.rsqrt(jnp.mean(c_kv * c_kv, axis=-1, keepdims=True) + RMS_EPS) * kvg_ref[...]
    qn, kvn = qn.astype(BF16), kvn.astype(BF16)

    cq, sqlo, sqhi = cq_ref[...], sqlo_ref[...], sqhi_ref[...]
    scale = (MLA_NOPE + MLA_ROPE) ** -0.5 * LOG2E
    for h in range(MLA_HEADS):
        sl = slice(h * MLA_QK_PAD, (h + 1) * MLA_QK_PAD)
        qh = _rope_lanes(_dot(qn, wq_ref[:, sl]), cq, sqlo, sqhi, MLA_ROPE // 2)
        q_out[:, sl] = (qh * scale).astype(BF16)

    k_pe = _rope_lanes(k_rope, ck_ref[...], sklo_ref[...], skhi_ref[...], MLA_ROPE // 2).astype(BF16)
    k_out[...] = (_dot(kvn, wk_ref[...]) + _dot(k_pe, e_ref[...])).astype(BF16)
    v_out[...] = _dot_nt(wv_ref[...], kvn).astype(BF16)


def _b_projection(x2, w_in, q_g, kv_g, wq, wk, wv, e_mat, tabs, S):
    n = x2.shape[0]
    tm = ROW_TILE
    per_seq = S // tm
    full = lambda a: pl.BlockSpec(a.shape, lambda i: (0,) * a.ndim)
    tab_spec = pl.BlockSpec((tm, LANES), lambda i: (i % per_seq, 0))
    row = lambda w: pl.BlockSpec((tm, w), lambda i: (i, 0))
    hw = MLA_HEADS * MLA_QK_PAD
    return pl.pallas_call(
        _b_proj_kernel,
        grid=(n // tm,),
        in_specs=[row(D_MODEL), full(w_in), full(q_g), full(kv_g), full(wq), full(wk), full(wv), full(e_mat)]
                 + [tab_spec] * 6,
        out_specs=[row(hw), row(hw), pl.BlockSpec((MLA_HEADS * MLA_V, tm), lambda i: (0, i)), row(MEM_WIDTH)],
        out_shape=[jax.ShapeDtypeStruct((n, hw), BF16), jax.ShapeDtypeStruct((n, hw), BF16),
                   jax.ShapeDtypeStruct((MLA_HEADS * MLA_V, n), BF16),
                   jax.ShapeDtypeStruct((n, MEM_WIDTH), BF16)],
        compiler_params=_params(1),
        name="b_projection",
    )(x2, w_in, q_g, kv_g, wq, wk, wv, e_mat, *tabs)


def _mla_kernel(q_ref, k_ref, vt_ref, o_ref, s_scr, *, S):
    tq = q_ref.shape[0]
    tk = MLA_TK
    n_chunks = S // tk

    def score_chunk(h, c, mx8):
        sl = slice(h * MLA_QK_PAD, (h + 1) * MLA_QK_PAD)
        s = _dot_nt(k_ref[c * tk:(c + 1) * tk, sl], q_ref[:, sl])
        s_scr[h % 2, c * tk:(c + 1) * tk, :] = s
        return jnp.maximum(mx8, jnp.max(s.reshape(tk // 8, 8, tq), axis=0))

    def value_chunk(h, c, m, l8, acc):
        p = jnp.exp2(s_scr[h % 2, c * tk:(c + 1) * tk, :] - m)
        l8 = l8 + jnp.sum(p.reshape(tk // 8, 8, tq), axis=0)
        acc = acc + _dot(vt_ref[h * MLA_V:(h + 1) * MLA_V, c * tk:(c + 1) * tk], p.astype(BF16))
        return l8, acc

    mx8 = jnp.full((8, tq), NEG_INF, F32)
    for c in range(n_chunks):
        mx8 = score_chunk(0, c, mx8)
    outs = []
    for h in range(MLA_HEADS):
        m = jnp.max(mx8, axis=0, keepdims=True)
        mx8 = jnp.full((8, tq), NEG_INF, F32)
        l8 = jnp.zeros((8, tq), F32)
        acc = jnp.zeros((MLA_V, tq), F32)
        for c in range(n_chunks):
            if h + 1 < MLA_HEADS:
                mx8 = score_chunk(h + 1, c, mx8)
            l8, acc = value_chunk(h, c, m, l8, acc)
        outs.append(acc / jnp.sum(l8, axis=0, keepdims=True))
    for pair in range(MLA_HEADS // 2):
        o_ref[:, pair * LANES:(pair + 1) * LANES] = jnp.concatenate(outs[2 * pair:2 * pair + 2], axis=0).T.astype(BF16)


def _mla_attention(q, k, vt, B, S):
    n = B * S
    per_seq = S // MLA_TQ
    return pl.pallas_call(
        functools.partial(_mla_kernel, S=S),
        grid=(B, per_seq),
        in_specs=[pl.BlockSpec((MLA_TQ, MLA_HEADS * MLA_QK_PAD), lambda b, i: (b * per_seq + i, 0)),
                  pl.BlockSpec((S, MLA_HEADS * MLA_QK_PAD), lambda b, i: (b, 0)),
                  pl.BlockSpec((MLA_HEADS * MLA_V, S), lambda b, i: (0, b))],
        out_specs=pl.BlockSpec((MLA_TQ, MLA_HEADS * MLA_V), lambda b, i: (b * per_seq + i, 0)),
        out_shape=jax.ShapeDtypeStruct((n, MLA_HEADS * MLA_V), BF16),
        scratch_shapes=[pltpu.VMEM((2, S, MLA_TQ), F32)],
        compiler_params=_params(2),
        name="mla_attention",
    )(q, k, vt)


def _cast_matmul_kernel(x_ref, w_ref, o_ref):
    o_ref[...] = _dot(x_ref[...].astype(BF16), w_ref[...]).astype(o_ref.dtype)


def _mem_kv(mem2, w):
    n, d = mem2.shape
    tm = min(ROW_TILE, n)
    return pl.pallas_call(
        _cast_matmul_kernel,
        grid=(n // tm,),
        in_specs=[pl.BlockSpec((tm, d), lambda i: (i, 0)), pl.BlockSpec(w.shape, lambda i: (0, 0))],
        out_specs=pl.BlockSpec((tm, w.shape[1]), lambda i: (i, 0)),
        out_shape=jax.ShapeDtypeStruct((n, w.shape[1]), BF16),
        compiler_params=_params(1),
        name="mem_kv",
    )(mem2, w)


def _mem_attn_kernel(q_ref, kv_ref, o_ref):
    tq = q_ref.shape[0]
    lane = lax.broadcasted_iota(jnp.int32, (tq, LANES), 1)
    for pair in range(MEM_WIDTH // LANES):
        sl = slice(pair * LANES, (pair + 1) * LANES)
        qp = q_ref[:, sl]
        kp = kv_ref[:, sl]
        vp = kv_ref[:, MEM_WIDTH + pair * LANES:MEM_WIDTH + (pair + 1) * LANES]
        o_pair = jnp.zeros((tq, LANES), F32)
        for hh in range(LANES // HEAD_DIM):
            own = (lane >= hh * HEAD_DIM) & (lane < (hh + 1) * HEAD_DIM)
            s = _dot_nt(jnp.where(own, qp, jnp.zeros_like(qp)), kp)
            m = jnp.max(s, axis=-1, keepdims=True)
            p = jnp.exp(s - m)
            l = jnp.sum(p, axis=-1, keepdims=True)
            o_pair = jnp.where(own, _dot(p.astype(BF16), vp) / l, o_pair)
        o_ref[:, sl] = o_pair.astype(BF16)


def _mem_attention(qm, kv, B, S, M):
    n = B * S
    per_seq = S // MEM_TQ
    return pl.pallas_call(
        _mem_attn_kernel,
        grid=(n // MEM_TQ,),
        in_specs=[pl.BlockSpec((MEM_TQ, MEM_WIDTH), lambda i: (i, 0)),
                  pl.BlockSpec((M, 2 * MEM_WIDTH), lambda i: (i // per_seq, 0))],
        out_specs=pl.BlockSpec((MEM_TQ, MEM_WIDTH), lambda i: (i, 0)),
        out_shape=jax.ShapeDtypeStruct((n, MEM_WIDTH), BF16),
        compiler_params=_params(1),
        name="mem_attention",
    )(qm, kv)


def _pack_bf16_pairs(x):
    w = x.shape[1] // 2
    hi = lax.bitcast_convert_type(x[:, :w].astype(BF16).astype(F32), jnp.uint32)
    lo = lax.bitcast_convert_type(x[:, w:].astype(BF16).astype(F32), jnp.uint32)
    return hi | (lo >> 16)


def _unpack_bf16_pairs(w):
    hi = lax.bitcast_convert_type(w & jnp.uint32(0xFFFF0000), F32)
    lo = lax.bitcast_convert_type(w << 16, F32)
    return hi, lo


def _route(logits, tri, run):
    lane = lax.broadcasted_iota(jnp.int32, logits.shape, 1).astype(F32)
    big = jnp.float32(1e9)
    is_grp = lane < N_GROUPS
    gl = jnp.where(is_grp, logits, NEG_INF)
    gmax = jnp.max(gl, axis=-1, keepdims=True)
    gidx = jnp.min(jnp.where(gl >= gmax, lane, big), axis=-1, keepdims=True)
    gsum = jnp.sum(jnp.where(is_grp, jnp.exp(gl - gmax), 0.0), axis=-1, keepdims=True)
    g_gate = 1.0 / gsum
    lo = N_GROUPS + EXPERTS_PER_GROUP * gidx
    el = jnp.where((lane >= lo) & (lane < lo + EXPERTS_PER_GROUP), logits, NEG_INF)
    v1 = jnp.max(el, axis=-1, keepdims=True)
    i1 = jnp.min(jnp.where(el >= v1, lane, big), axis=-1, keepdims=True)
    el2 = jnp.where(lane == i1, NEG_INF, el)
    v2 = jnp.max(el2, axis=-1, keepdims=True)
    i2 = jnp.min(jnp.where(el2 >= v2, lane, big), axis=-1, keepdims=True)
    e2 = jnp.exp(v2 - v1)
    den = 1.0 + e2
    gate1 = g_gate / den
    gate2 = g_gate * e2 / den
    e1 = i1 - N_GROUPS
    e2 = i2 - N_GROUPS
    oh1 = lane == e1
    oh2 = lane == e2
    member = jnp.where(oh1 | oh2, 1.0, 0.0)
    before = _dot(tri, member.astype(BF16)) + run
    rank1 = jnp.sum(jnp.where(oh1, before, 0.0), axis=-1, keepdims=True)
    rank2 = jnp.sum(jnp.where(oh2, before, 0.0), axis=-1, keepdims=True)
    slab = jnp.zeros_like(logits)
    for k, col in enumerate((e1, e2, gate1, gate2, rank1, rank2)):
        slab = jnp.where(lane == k, col, slab)
    return slab, jnp.sum(member, axis=0, keepdims=True)


def _post_kernel(x_ref, mix_ref, mo_ref, w1_ref, w2_ref, g_ref, b_ref, wr_ref, br_ref, tri_ref,
                 xn_ref, xw_ref, route_ref, cnt_ref, run_ref):
    @pl.when(pl.program_id(0) == 0)
    def _():
        run_ref[...] = jnp.zeros_like(run_ref)

    y = _dot(mix_ref[...], w1_ref[...]) + _dot(mo_ref[...], w2_ref[...])
    xn = _layer_norm(ALPHA * x_ref[...] + y, g_ref[...], b_ref[...])
    xn_ref[...] = xn
    xw_ref[...] = _pack_bf16_pairs(xn)
    slab, counts = _route(_dot(xn.astype(BF16), wr_ref[...]) + br_ref[...], tri_ref[...], run_ref[0:1, :])
    route_ref[...] = slab
    run_ref[...] = run_ref[...] + counts
    cnt_ref[...] = run_ref[...]


def _post_mixer(x2, mix, mo, w1, w2, g, b, wr, br):
    n = x2.shape[0]
    tm = ROW_TILE
    full = lambda a: pl.BlockSpec(a.shape, lambda i: (0,) * a.ndim)
    row = lambda w: pl.BlockSpec((tm, w), lambda i: (i, 0))
    idx = jnp.arange(tm)
    tri = (idx[:, None] > idx[None, :]).astype(BF16)
    return pl.pallas_call(
        _post_kernel,
        grid=(n // tm,),
        in_specs=[row(D_MODEL), row(mix.shape[1]), row(mo.shape[1]), full(w1), full(w2), full(g), full(b),
                  full(wr), full(br), full(tri)],
        out_specs=[row(D_MODEL), row(D_MODEL // 2), row(LANES), pl.BlockSpec((8, LANES), lambda i: (0, 0))],
        out_shape=[jax.ShapeDtypeStruct((n, D_MODEL), F32), jax.ShapeDtypeStruct((n, D_MODEL // 2), jnp.uint32),
                   jax.ShapeDtypeStruct((n, LANES), F32), jax.ShapeDtypeStruct((8, LANES), F32)],
        scratch_shapes=[pltpu.VMEM((8, LANES), F32)],
        compiler_params=_params(1),
        name="post_mixer",
    )(x2, mix, mo, w1, w2, g, b, wr, br, tri)


def _row_copy(src, src_row, dst, dst_row, sem):
    return pltpu.make_async_copy(src.at[pl.ds(src_row, 1), :], dst.at[pl.ds(dst_row, 1), :], sem)


def _dispatch_kernel(dest_ref, xw_ref, xs_in_ref, xs_ref, sem):
    del xs_in_ref
    tm = xw_ref.shape[0]

    def issue(r, carry):
        for k in range(TOP_K):
            _row_copy(xw_ref, r, xs_ref, dest_ref[0, k, r], sem).start(priority=k)
        return carry

    lax.fori_loop(0, tm, issue, 0, unroll=8)

    def drain(r, carry):
        for k in range(TOP_K):
            _row_copy(xw_ref, 0, xs_ref, 0, sem).wait()
        return carry

    lax.fori_loop(0, tm, drain, 0, unroll=8)


def _dispatch(xw, dest, n_slots):
    n, w = xw.shape
    tm = ROW_TILE
    return pl.pallas_call(
        _dispatch_kernel,
        grid=(n // tm,),
        in_specs=[pl.BlockSpec((1, TOP_K, tm), lambda i: (i, 0, 0), memory_space=pltpu.SMEM),
                  pl.BlockSpec((tm, w), lambda i: (i, 0)),
                  pl.BlockSpec(memory_space=pl.ANY)],
        out_specs=pl.BlockSpec(memory_space=pl.ANY),
        out_shape=jax.ShapeDtypeStruct((n_slots, w), jnp.uint32),
        scratch_shapes=[pltpu.SemaphoreType.DMA(())],
        input_output_aliases={2: 0},
        compiler_params=_params(1),
        name="moe_dispatch",
    )(dest, xw, jnp.zeros((n_slots, w), jnp.uint32))


def _expert_kernel(blk_e_ref, n_used_ref, xs_ref, wg_ref, wu_ref, wd_ref, ys_ref, wg_s, wu_s, wd_s):
    i = pl.program_id(0)
    half = D_MODEL // 2

    @pl.when((i == 0) | (blk_e_ref[i] != blk_e_ref[jnp.maximum(i - 1, 0)]))
    def _():
        wg_s[...] = wg_ref[0].astype(BF16)
        wu_s[...] = wu_ref[0].astype(BF16)
        wd_s[...] = wd_ref[0].astype(BF16)

    @pl.when(i < n_used_ref[0])
    def _():
        hi, lo = _unpack_bf16_pairs(xs_ref[...])
        hi, lo = hi.astype(BF16), lo.astype(BF16)
        gate = _dot(hi, wg_s[:half, :]) + _dot(lo, wg_s[half:, :])
        up = _dot(hi, wu_s[:half, :]) + _dot(lo, wu_s[half:, :])
        h = (gate * jax.nn.sigmoid(gate) * up).astype(BF16)
        ys_ref[...] = _pack_bf16_pairs(_dot(h, wd_s[...]))

    @pl.when(i >= n_used_ref[0])
    def _():
        ys_ref[...] = jnp.zeros_like(ys_ref)


def _experts(xs, blk_expert, n_used, wg, wu, wd):
    n_slots, w = xs.shape
    tm = MOE_TILE
    grid_spec = pltpu.PrefetchScalarGridSpec(
        num_scalar_prefetch=2,
        grid=(n_slots // tm,),
        in_specs=[pl.BlockSpec((tm, w), lambda i, be, nu: (i, 0)),
                  pl.BlockSpec((1, D_MODEL, D_EXPERT), lambda i, be, nu: (be[i], 0, 0)),
                  pl.BlockSpec((1, D_MODEL, D_EXPERT), lambda i, be, nu: (be[i], 0, 0)),
                  pl.BlockSpec((1, D_EXPERT, D_MODEL), lambda i, be, nu: (be[i], 0, 0))],
        out_specs=pl.BlockSpec((tm, w), lambda i, be, nu: (i, 0)),
        scratch_shapes=[pltpu.VMEM((D_MODEL, D_EXPERT), BF16), pltpu.VMEM((D_MODEL, D_EXPERT), BF16),
                        pltpu.VMEM((D_EXPERT, D_MODEL), BF16)],
    )
    return pl.pallas_call(
        _expert_kernel,
        grid_spec=grid_spec,
        out_shape=jax.ShapeDtypeStruct((n_slots, w), jnp.uint32),
        compiler_params=_params(1),
        name="experts",
    )(blk_expert, n_used, xs, wg, wu, wd)


def _ffn_ln_kernel(dest_ref, xn_ref, route_ref, g_ref, b_ref, ys_ref, o_ref, ybuf, sem):
    tm = xn_ref.shape[0]

    def issue(r, carry):
        for k in range(TOP_K):
            _row_copy(ys_ref, dest_ref[0, k, r], ybuf.at[k], r, sem).start(priority=k)
        return carry

    lax.fori_loop(0, tm, issue, 0, unroll=8)

    def drain(r, carry):
        for k in range(TOP_K):
            _row_copy(ys_ref, 0, ybuf.at[k], 0, sem).wait()
        return carry

    lax.fori_loop(0, tm, drain, 0, unroll=8)

    route = route_ref[...]
    g1 = route[:, 2:3]
    g2 = route[:, 3:4]
    hi1, lo1 = _unpack_bf16_pairs(ybuf[0])
    hi2, lo2 = _unpack_bf16_pairs(ybuf[1])
    f = jnp.concatenate([g1 * hi1 + g2 * hi2, g1 * lo1 + g2 * lo2], axis=1)
    o_ref[...] = _layer_norm(ALPHA * xn_ref[...] + f, g_ref[...], b_ref[...])


def _ffn_ln(xn, ys, dest, route, g, b):
    n = xn.shape[0]
    tm = ROW_TILE
    full = lambda a: pl.BlockSpec(a.shape, lambda i: (0,) * a.ndim)
    row = lambda w: pl.BlockSpec((tm, w), lambda i: (i, 0))
    return pl.pallas_call(
        _ffn_ln_kernel,
        grid=(n // tm,),
        in_specs=[pl.BlockSpec((1, TOP_K, tm), lambda i: (i, 0, 0), memory_space=pltpu.SMEM),
                  row(D_MODEL), row(LANES), full(g), full(b), pl.BlockSpec(memory_space=pl.ANY)],
        out_specs=row(D_MODEL),
        out_shape=jax.ShapeDtypeStruct((n, D_MODEL), F32),
        scratch_shapes=[pltpu.VMEM((TOP_K, tm, ys.shape[1]), jnp.uint32), pltpu.SemaphoreType.DMA(())],
        compiler_params=_params(1),
        name="ffn_ln",
    )(dest, xn, route, g, b, ys)


def _moe(xn, xw, route, counts, wg, wu, wd, ln_g, ln_b):
    n = xn.shape[0]
    tm = MOE_TILE
    n_blocks = n * TOP_K // tm + N_EXPERTS
    cnt = counts[0, :N_EXPERTS].astype(jnp.int32)
    padded = (cnt + tm - 1) // tm * tm
    pends = jnp.cumsum(padded)
    pstarts = pends - padded
    eid = route[:, :TOP_K].astype(jnp.int32)
    rank = route[:, 4:4 + TOP_K].astype(jnp.int32)
    dest = jnp.take(pstarts, eid, mode="clip") + rank
    dest = dest.reshape(n // ROW_TILE, ROW_TILE, TOP_K).transpose(0, 2, 1)
    blk_start = jnp.arange(n_blocks, dtype=jnp.int32) * tm
    blk_expert = jnp.minimum(jnp.sum(pends[None, :] <= blk_start[:, None], axis=1), N_EXPERTS - 1).astype(jnp.int32)
    n_used = (pends[-1] // tm).astype(jnp.int32).reshape(1)
    xs = _dispatch(xw, dest, n_blocks * tm)
    ys = _experts(xs, blk_expert, n_used, wg, wu, wd)
    return _ffn_ln(xn, ys, dest, route, ln_g, ln_b)


def _rope_tables(S, dim):
    inv = ROPE_THETA ** (-jnp.arange(0, dim, 2, dtype=F32) / dim)
    ang = jnp.arange(S, dtype=F32)[:, None] * inv[None, :]
    return jnp.cos(ang), jnp.sin(ang)


def _lane_tables(cos, sin, first, period):
    S, half = cos.shape
    z = jnp.zeros((S, half), F32)

    def build(rot_lo, rot_hi, fill):
        unit = jnp.concatenate([jnp.full((S, first), fill, F32), rot_lo, rot_hi,
                                jnp.full((S, period - first - 2 * half), fill, F32)], axis=1)
        return jnp.tile(unit, (1, LANES // period))

    return build(cos, cos, 1.0), build(-sin, z, 0.0), build(z, sin, 0.0)


def _router_weights(w_grp, b_grp, w_exp, b_exp):
    pad = LANES - N_GROUPS - N_EXPERTS
    wr = jnp.concatenate([w_grp, w_exp, jnp.zeros((D_MODEL, pad), F32)], axis=1).astype(BF16)
    br = jnp.concatenate([b_grp, b_exp, jnp.zeros((pad,), F32)]).reshape(1, LANES)
    return wr, br


def kernel(x, mem, a_w_in, a_w_out, b_w_in, b_q_norm, b_w_qb, b_kv_norm, b_w_kvb, b_w_out, mem_w_kv, ln_mix_g, ln_mix_b, ln_ffn_g, ln_ffn_b, router_w_grp, router_b_grp, router_w_exp, router_b_exp, exp_w_gate, exp_w_up, exp_w_down):
    B, S, D = x.shape
    M = mem.shape[1]
    n = B * S
    x2 = x.reshape(n, D)
    mem2 = mem.reshape(B * M, D)
    cos_a, sin_a = _rope_tables(S, ROT_DIM)
    cos_b, sin_b = _rope_tables(S, MLA_ROPE)
    tabs_a = _lane_tables(cos_a, sin_a, 0, HEAD_DIM)
    tabs_bq = _lane_tables(cos_b, sin_b, MLA_NOPE, MLA_QK_PAD)
    tabs_bk = _lane_tables(cos_b, sin_b, 0, LANES)
    row = lambda v: v.reshape(1, -1)

    for i in range(DEPTH):
        j = i // 2
        if i % 2 == 0:
            outs = _a_projection(x2, a_w_in[j].astype(BF16), tabs_a, B, S)
            ng = len(DIL_GROUPS)
            o_g, lse_g = [], []
            for g, (_, dil) in enumerate(DIL_GROUPS):
                T = S // dil
                sub = lambda t: t.reshape(B * dil, T, GROUP_WIDTH)
                o, lse = _band_attention(sub(outs[g]), sub(outs[ng + g]), sub(outs[2 * ng + g]))
                o_g.append(o.reshape(B, dil, T, GROUP_WIDTH))
                lse_g.append(lse.reshape(B, dil, T, GROUP_WIDTH))
            qm = outs[-1]
            mix = _a_merge(o_g, lse_g, B, S)
            w_out = a_w_out[j]
        else:
            w_in = b_w_in[j]
            c_q, c_kv, k_rope, w_qm = jnp.split(w_in, [Q_LORA, Q_LORA + KV_LORA, Q_LORA + KV_LORA + MLA_ROPE], axis=1)
            w_in_p = jnp.concatenate([c_q, w_qm, c_kv, k_rope, jnp.zeros((D, LANES - MLA_ROPE), F32)], axis=1)
            wq = jnp.pad(b_w_qb[j].reshape(Q_LORA, MLA_HEADS, MLA_NOPE + MLA_ROPE),
                         ((0, 0), (0, 0), (0, MLA_QK_PAD - MLA_NOPE - MLA_ROPE))).reshape(Q_LORA, -1)
            wkv = b_w_kvb[j].reshape(KV_LORA, MLA_HEADS, MLA_NOPE + MLA_V)
            wk = jnp.pad(wkv[:, :, :MLA_NOPE], ((0, 0), (0, 0), (0, MLA_QK_PAD - MLA_NOPE))).reshape(KV_LORA, -1)
            wv = wkv[:, :, MLA_NOPE:].reshape(KV_LORA, -1).T
            rows = jnp.arange(LANES)[:, None]
            cols = jnp.arange(MLA_HEADS * MLA_QK_PAD)[None, :]
            e_mat = ((rows < MLA_ROPE) & (cols % MLA_QK_PAD == rows + MLA_NOPE)).astype(BF16)
            q, k, v, qm = _b_projection(x2, w_in_p.astype(BF16), row(b_q_norm[j]), row(b_kv_norm[j]),
                                        wq.astype(BF16), wk.astype(BF16), wv.astype(BF16), e_mat,
                                        tabs_bq + tabs_bk, S)
            mix = _mla_attention(q, k, v, B, S)
            w_out = b_w_out[j]
        kv_m = _mem_kv(mem2, mem_w_kv[i].astype(BF16))
        mo = _mem_attention(qm, kv_m, B, S, M)
        wm = mix.shape[1]
        wr, br = _router_weights(router_w_grp[i], router_b_grp[i], router_w_exp[i], router_b_exp[i])
        xn, xw, route, counts = _post_mixer(x2, mix, mo, w_out[:wm].astype(BF16), w_out[wm:].astype(BF16),
                                            row(ln_mix_g[i]), row(ln_mix_b[i]), wr, br)
        x2 = _moe(xn, xw, route, counts, exp_w_gate[i], exp_w_up[i], exp_w_down[i],
                  row(ln_ffn_g[i]), row(ln_ffn_b[i]))
    return x2.reshape(B, S, D)
```
